```python
import math
import jax, jax.numpy as jnp
from jax import lax
import numpy as np


D_MODEL = 1024
BATCH = 8
SEQ = 4096
DEPTH = 2

N_EVEN = (DEPTH + 1) // 2
N_ODD = DEPTH // 2

A_HEADS = 8
A_HEAD_DIM = D_MODEL // 16
A_WIDTH = A_HEADS * A_HEAD_DIM
MOBA_BLOCK = 256
MOBA_TOPK = 3
MOBA_QCHUNK = 32
REL_BUCKETS = 32
REL_MAX_DIST = 128

B_HEADS = 4
B_HEAD_DIM = D_MODEL // 8
B_WIDTH = B_HEADS * B_HEAD_DIM
MLSTM_CONV = 4
MLSTM_CHUNK = 64

POOL_WINDOWS = (2, 4, 8, 16)
POOL_GROUPS = 4
POOL_GROUP_DIM = D_MODEL // 8
POOL_WIDTH = POOL_GROUPS * POOL_GROUP_DIM

R_HEADS = 4
R_QK_DIM = D_MODEL // 16
R_V_DIM = 2 * R_QK_DIM
R_QK_WIDTH = R_HEADS * R_QK_DIM
R_V_WIDTH = R_HEADS * R_V_DIM
RET_CHUNK = 64
ROPE_BASE = 10000.0

D_FF = 2816
FFN_CONV = 3

EPS = 1e-6
NEG = -1e30

AB_SIZES = (A_WIDTH, A_WIDTH, A_WIDTH, 2 * B_WIDTH, B_WIDTH, B_WIDTH, B_HEADS, B_HEADS)
CD_SIZES = (POOL_WIDTH, R_QK_WIDTH, R_QK_WIDTH, R_V_WIDTH, R_V_WIDTH)
IN_AB = 3 * A_WIDTH + 4 * B_WIDTH + 2 * B_HEADS
IN_CD = POOL_WIDTH + 2 * R_QK_WIDTH + 2 * R_V_WIDTH
MIX_AB = A_WIDTH + B_WIDTH
MIX_CD = POOL_WIDTH + R_V_WIDTH

kernel_name = 'hybrid_moba_mlstm_pool_retention'


def rmsnorm(x, g):
    xf = x.astype(jnp.float32)
    y = xf * lax.rsqrt(jnp.mean(xf * xf, axis=-1, keepdims=True) + EPS)
    return (y * g.astype(jnp.float32)).astype(x.dtype)


def head_rmsnorm(t, g):
    n_heads = t.shape[1]
    tf = t.astype(jnp.float32)
    y = tf * lax.rsqrt(jnp.mean(tf * tf, axis=-1, keepdims=True) + EPS)
    return y * g.astype(jnp.float32).reshape(n_heads, 1, -1)


def split_cols(z, sizes):
    idx = []
    acc = 0
    for s in sizes[:-1]:
        acc += s
        idx.append(acc)
    return jnp.split(z, idx, axis=-1)


def to_heads(t, n_heads):
    b, s, w = t.shape
    return t.reshape(b, s, n_heads, w // n_heads).transpose(0, 2, 1, 3)


def from_heads(t):
    b, h, s, d = t.shape
    return t.transpose(0, 2, 1, 3).reshape(b, s, h * d)


def causal_dwconv(x, w, b):
    k = w.shape[0]
    c = x.shape[-1]
    y = lax.conv_general_dilated(x, w[:, None, :].astype(x.dtype), window_strides=(1,), padding=[(k - 1, 0)],
                                 dimension_numbers=('NWC', 'WIO', 'NWC'), feature_group_count=c)
    return y + b.astype(x.dtype)


def rel_bucket(dist):
    max_exact = REL_BUCKETS // 2
    d = jnp.maximum(dist, 0)
    df = jnp.maximum(d, 1).astype(jnp.float32)
    large = max_exact + (jnp.log(df / max_exact) / math.log(REL_MAX_DIST / max_exact)
                         * (REL_BUCKETS - max_exact)).astype(jnp.int32)
    large = jnp.minimum(large, REL_BUCKETS - 1)
    return jnp.where(d < max_exact, d, large)


def moba_attention(q, k, v, rel_bias):
    bsz, n_h, s, dh = q.shape
    nb = -(-s // MOBA_BLOCK)
    pad = nb * MOBA_BLOCK - s
    kb = jnp.pad(k, ((0, 0), (0, 0), (0, pad), (0, 0))).reshape(bsz, n_h, nb, MOBA_BLOCK, dh)
    vb = jnp.pad(v, ((0, 0), (0, 0), (0, pad), (0, 0))).reshape(bsz, n_h, nb, MOBA_BLOCK, dh)
    kmean = jnp.mean(kb.astype(jnp.float32), axis=3)
    n_sel = min(MOBA_TOPK, nb - 1)
    n_chunks = s // MOBA_QCHUNK
    scale = dh ** -0.5
    bias_tab = rel_bias.astype(jnp.float32).T
    blk_off = jnp.arange(MOBA_BLOCK)
    b_idx = jnp.arange(bsz)[:, None, None, None]
    h_idx = jnp.arange(n_h)[None, :, None, None]
    h_idx5 = jnp.arange(n_h)[None, :, None, None, None]

    def chunk(c):
        q0 = c * MOBA_QCHUNK
        qc = lax.dynamic_slice_in_dim(q, q0, MOBA_QCHUNK, axis=2).astype(jnp.float32)
        qpos = q0 + jnp.arange(MOBA_QCHUNK)
        cur = q0 // MOBA_BLOCK
        k_own = lax.dynamic_index_in_dim(kb, cur, axis=2, keepdims=False).astype(jnp.float32)
        v_own = lax.dynamic_index_in_dim(vb, cur, axis=2, keepdims=False).astype(jnp.float32)
        dist_own = qpos[:, None] - (cur * MOBA_BLOCK + blk_off)[None, :]
        s_own = jnp.einsum('bhqd,bhkd->bhqk', qc, k_own) * scale + bias_tab[:, rel_bucket(dist_own)]
        s_own = jnp.where(dist_own >= 0, s_own, NEG)
        if n_sel == 0:
            p_own = jax.nn.softmax(s_own, axis=-1)
            return jnp.einsum('bhqk,bhkd->bhqd', p_own, v_own).astype(q.dtype)
        gate = jnp.einsum('bhqd,bhnd->bhqn', qc, kmean)
        gate = jnp.where(jnp.arange(nb) < cur, gate, NEG)
        _, idx = lax.top_k(gate, n_sel)
        valid = jnp.arange(n_sel) < cur
        k_sel = kb[b_idx, h_idx, idx].astype(jnp.float32)
        v_sel = vb[b_idx, h_idx, idx].astype(jnp.float32)
        dist_sel = qpos[None, None, :, None, None] - (idx[..., None] * MOBA_BLOCK + blk_off)
        s_sel = jnp.einsum('bhqd,bhqnkd->bhqnk', qc, k_sel) * scale + bias_tab[h_idx5, rel_bucket(dist_sel)]
        s_sel = jnp.where(valid[:, None], s_sel, NEG).reshape(bsz, n_h, MOBA_QCHUNK, n_sel * MOBA_BLOCK)
        p = jax.nn.softmax(jnp.concatenate([s_sel, s_own], axis=-1), axis=-1)
        p_sel = p[..., :n_sel * MOBA_BLOCK].reshape(bsz, n_h, MOBA_QCHUNK, n_sel, MOBA_BLOCK)
        p_own = p[..., n_sel * MOBA_BLOCK:]
        out = jnp.einsum('bhqnk,bhqnkd->bhqd', p_sel, v_sel) + jnp.einsum('bhqk,bhkd->bhqd', p_own, v_own)
        return out.astype(q.dtype)

    outs = lax.map(chunk, jnp.arange(n_chunks))
    return outs.transpose(1, 2, 0, 3, 4).reshape(bsz, n_h, s, dh)


def mlstm(q, k, v, i_pre, f_pre):
    bsz, n_h, s, dh = q.shape
    L = MLSTM_CHUNK
    nc = s // L
    def chunks(a):
        return jnp.moveaxis(a.reshape((bsz, n_h, nc, L) + a.shape[3:]), 2, 0)
    qf = chunks(q.astype(jnp.float32))
    kf = chunks(k.astype(jnp.float32) * (dh ** -0.5))
    vf = chunks(v.astype(jnp.float32))
    ig = chunks(i_pre.astype(jnp.float32))
    lf = chunks(jax.nn.log_sigmoid(f_pre.astype(jnp.float32)))
    causal = jnp.tril(jnp.ones((L, L), dtype=bool))

    def step(carry, inp):
        C, n, m = carry
        qc, kc, vc, ic, fc = inp
        bcum = jnp.cumsum(fc, axis=-1)
        log_inter = bcum + m[..., None]
        log_intra = jnp.where(causal, bcum[..., :, None] - bcum[..., None, :] + ic[..., None, :], -jnp.inf)
        m_t = jnp.maximum(log_inter, jnp.max(log_intra, axis=-1))
        w_inter = jnp.exp(log_inter - m_t)
        qk = jnp.einsum('bhtd,bhsd->bhts', qc, kc) * jnp.exp(log_intra - m_t[..., None])
        num = w_inter[..., None] * jnp.einsum('bhvk,bhtk->bhtv', C, qc) + jnp.einsum('bhts,bhsv->bhtv', qk, vc)
        den = w_inter * jnp.einsum('bhk,bhtk->bht', n, qc) + jnp.sum(qk, axis=-1)
        h = num / jnp.maximum(jnp.abs(den), jnp.exp(-m_t))[..., None]
        b_last = bcum[..., -1]
        log_s = b_last[..., None] - bcum + ic
        m_new = jnp.maximum(b_last + m, jnp.max(log_s, axis=-1))
        w_prev = jnp.exp(b_last + m - m_new)
        w_s = jnp.exp(log_s - m_new[..., None])
        C_new = w_prev[..., None, None] * C + jnp.einsum('bhsv,bhsk->bhvk', vc * w_s[..., None], kc)
        n_new = w_prev[..., None] * n + jnp.einsum('bhs,bhsk->bhk', w_s, kc)
        return (C_new, n_new, m_new), h

    init = (jnp.zeros((bsz, n_h, dh, dh), jnp.float32), jnp.zeros((bsz, n_h, dh), jnp.float32),
            jnp.zeros((bsz, n_h), jnp.float32))
    _, hs = lax.scan(step, init, (qf, kf, vf, ig, lf))
    return jnp.moveaxis(hs, 0, 2).reshape(bsz, n_h, s, dh)


def rope(x):
    s, d = x.shape[2], x.shape[3]
    half = d // 2
    inv = ROPE_BASE ** (-jnp.arange(half, dtype=jnp.float32) / half)
    ang = jnp.arange(s, dtype=jnp.float32)[:, None] * inv[None, :]
    cos, sin = jnp.cos(ang), jnp.sin(ang)
    x1, x2 = x[..., :half], x[..., half:]
    return jnp.concatenate([x1 * cos - x2 * sin, x1 * sin + x2 * cos], axis=-1)


def retention(q, k, v):
    bsz, n_h, s, dk = q.shape
    dv = v.shape[-1]
    L = RET_CHUNK
    nc = s // L
    log_g = jnp.log(1.0 - jnp.exp2(-5.0 - jnp.arange(n_h, dtype=jnp.float32)))
    t = jnp.arange(L, dtype=jnp.float32)
    rel = t[:, None] - t[None, :]
    decay_intra = jnp.where(rel >= 0, jnp.exp(jnp.maximum(rel, 0.0) * log_g[:, None, None]), 0.0)
    decay_q = jnp.exp((t + 1.0) * log_g[:, None])
    decay_k = jnp.exp((L - 1.0 - t) * log_g[:, None])
    decay_chunk = jnp.exp(L * log_g)
    def chunks(a):
        return jnp.moveaxis(a.reshape(bsz, n_h, nc, L, a.shape[-1]), 2, 0)

    def step(R, inp):
        qc, kc, vc = inp
        intra = jnp.einsum('bhts,bhsv->bhtv', jnp.einsum('bhtd,bhsd->bhts', qc, kc) * decay_intra, vc)
        inter = jnp.einsum('bhtd,bhdv->bhtv', qc * decay_q[..., None], R)
        R_new = decay_chunk[:, None, None] * R + jnp.einsum('bhsd,bhsv->bhdv', kc * decay_k[..., None], vc)
        return R_new, intra + inter

    R0 = jnp.zeros((bsz, n_h, dk, dv), jnp.float32)
    _, ys = lax.scan(step, R0, (chunks(q), chunks(k), chunks(v.astype(jnp.float32))))
    return jnp.moveaxis(ys, 0, 2).reshape(bsz, n_h, s, dv)


def multiscale_pool(u, pool_w, pool_scale):
    bsz, s, _ = u.shape
    ug = u.astype(jnp.float32).reshape(bsz, s, POOL_GROUPS, POOL_GROUP_DIM)
    cs = jnp.concatenate([jnp.zeros((bsz, 1, POOL_GROUPS, POOL_GROUP_DIM), jnp.float32),
                          jnp.cumsum(ug, axis=1)], axis=1)
    pos = jnp.arange(s)
    outs = []
    for g, w in enumerate(POOL_WINDOWS):
        start = jnp.maximum(pos + 1 - w, 0)
        cnt = (pos + 1 - start).astype(jnp.float32)
        mean = (cs[:, 1:, g] - cs[:, start, g]) / cnt[None, :, None]
        outs.append(mean - ug[:, :, g])
    pooled = jnp.stack(outs, axis=2)
    mixed = jnp.einsum('bsgc,gcd->bsgd', pooled, pool_w.astype(jnp.float32))
    return mixed.reshape(bsz, s, POOL_WIDTH) * pool_scale.astype(jnp.float32)


def mixer_ab(y, w_in, w_out, conv_w, conv_b, b_i, b_f, mlstm_g, rel_bias):
    z = y @ w_in
    aq, ak, av, bqk, bv, bo, bi, bf = split_cols(z, AB_SIZES)
    ya = moba_attention(to_heads(aq, A_HEADS), to_heads(ak, A_HEADS), to_heads(av, A_HEADS), rel_bias)
    bqk = jax.nn.silu(causal_dwconv(bqk, conv_w, conv_b))
    bq, bk = jnp.split(bqk, 2, axis=-1)
    ig = (bi + b_i).transpose(0, 2, 1)
    fg = (bf + b_f).transpose(0, 2, 1)
    hb = mlstm(to_heads(bq, B_HEADS), to_heads(bk, B_HEADS), to_heads(bv, B_HEADS), ig, fg)
    yb = from_heads(head_rmsnorm(hb, mlstm_g)) * jax.nn.sigmoid(bo.astype(jnp.float32))
    cat = jnp.concatenate([from_heads(ya).astype(y.dtype), yb.astype(y.dtype)], axis=-1)
    return cat @ w_out


def mixer_cd(y, w_in, w_out, pool_w, pool_scale, ret_g):
    z = y @ w_in
    pu, rq, rk, rv, rg = split_cols(z, CD_SIZES)
    yc = multiscale_pool(pu, pool_w, pool_scale)
    q = rope(to_heads(rq, R_HEADS).astype(jnp.float32))
    k = rope(to_heads(rk, R_HEADS).astype(jnp.float32)) * (R_QK_DIM ** -0.5)
    hd = retention(q, k, to_heads(rv, R_HEADS))
    yd = from_heads(head_rmsnorm(hd, ret_g)) * jax.nn.silu(rg.astype(jnp.float32))
    cat = jnp.concatenate([yc.astype(y.dtype), yd.astype(y.dtype)], axis=-1)
    return cat @ w_out


def conv_ffn(y, w_up, conv_w, conv_b, w_down):
    u = causal_dwconv(y @ w_up, conv_w, conv_b)
    gate, val = jnp.split(u, 2, axis=-1)
    return (jax.nn.silu(gate) * val) @ w_down


def setup_inputs(seed: int = 0) -> dict:
    key = jax.random.key(seed)
    ks = jax.random.split(key, 24)
    f32 = jnp.float32
    def nrm(k, shape, scale):
        return jax.random.normal(k, shape, f32) * scale
    x = nrm(ks[0], (BATCH, SEQ, D_MODEL), 1.0)
    rel_bias = nrm(ks[1], (REL_BUCKETS, A_HEADS), 0.5)
    norm_mix_g = 1.0 + nrm(ks[2], (DEPTH, D_MODEL), 0.02)
    norm_ffn_g = 1.0 + nrm(ks[3], (DEPTH, D_MODEL), 0.02)
    norm_out_g = 1.0 + nrm(ks[4], (D_MODEL,), 0.02)
    w_in_ab = nrm(ks[5], (N_EVEN, D_MODEL, IN_AB), D_MODEL ** -0.5)
    w_out_ab = nrm(ks[6], (N_EVEN, MIX_AB, D_MODEL), MIX_AB ** -0.5)
    mlstm_conv_w = nrm(ks[7], (N_EVEN, MLSTM_CONV, 2 * B_WIDTH), MLSTM_CONV ** -0.5)
    mlstm_conv_b = nrm(ks[8], (N_EVEN, 2 * B_WIDTH), 0.01)
    mlstm_b_i = nrm(ks[9], (N_EVEN, B_HEADS), 0.1)
    mlstm_b_f = jnp.linspace(3.0, 6.0, B_HEADS, dtype=f32)[None, :] + nrm(ks[10], (N_EVEN, B_HEADS), 0.01)
    mlstm_norm_g = 1.0 + nrm(ks[11], (N_EVEN, B_WIDTH), 0.02)
    w_in_cd = nrm(ks[12], (N_ODD, D_MODEL, IN_CD), D_MODEL ** -0.5)
    w_out_cd = nrm(ks[13], (N_ODD, MIX_CD, D_MODEL), MIX_CD ** -0.5)
    pool_w = nrm(ks[14], (N_ODD, POOL_GROUPS, POOL_GROUP_DIM, POOL_GROUP_DIM), POOL_GROUP_DIM ** -0.5)
    pool_scale = 1.0 + nrm(ks[15], (N_ODD, POOL_WIDTH), 0.02)
    ret_norm_g = 1.0 + nrm(ks[16], (N_ODD, R_V_WIDTH), 0.02)
    ffn_w_up = nrm(ks[17], (DEPTH, D_MODEL, 2 * D_FF), D_MODEL ** -0.5)
    ffn_conv_w = nrm(ks[18], (DEPTH, FFN_CONV, 2 * D_FF), FFN_CONV ** -0.5)
    ffn_conv_b = nrm(ks[19], (DEPTH, 2 * D_FF), 0.01)
    ffn_w_down = nrm(ks[20], (DEPTH, D_FF, D_MODEL), D_FF ** -0.5)
    return {'x': x, 'rel_bias': rel_bias, 'norm_mix_g': norm_mix_g, 'norm_ffn_g': norm_ffn_g,
            'norm_out_g': norm_out_g, 'w_in_ab': w_in_ab, 'w_out_ab': w_out_ab,
            'mlstm_conv_w': mlstm_conv_w, 'mlstm_conv_b': mlstm_conv_b, 'mlstm_b_i': mlstm_b_i,
            'mlstm_b_f': mlstm_b_f, 'mlstm_norm_g': mlstm_norm_g, 'w_in_cd': w_in_cd, 'w_out_cd': w_out_cd,
            'pool_w': pool_w, 'pool_scale': pool_scale, 'ret_norm_g': ret_norm_g, 'ffn_w_up': ffn_w_up,
            'ffn_conv_w': ffn_conv_w, 'ffn_conv_b': ffn_conv_b, 'ffn_w_down': ffn_w_down}


def reference(x, rel_bias, norm_mix_g, norm_ffn_g, norm_out_g, w_in_ab, w_out_ab, mlstm_conv_w, mlstm_conv_b,
              mlstm_b_i, mlstm_b_f, mlstm_norm_g, w_in_cd, w_out_cd, pool_w, pool_scale, ret_norm_g,
              ffn_w_up, ffn_conv_w, ffn_conv_b, ffn_w_down):
    h = x
    for layer in range(DEPTH):
        j = layer // 2
        y = rmsnorm(h, norm_mix_g[layer])
        if layer % 2 == 0:
            mix = mixer_ab(y, w_in_ab[j], w_out_ab[j], mlstm_conv_w[j], mlstm_conv_b[j], mlstm_b_i[j],
                           mlstm_b_f[j], mlstm_norm_g[j], rel_bias)
        else:
            mix = mixer_cd(y, w_in_cd[j], w_out_cd[j], pool_w[j], pool_scale[j], ret_norm_g[j])
        h = h + mix.astype(h.dtype)
        f = conv_ffn(rmsnorm(h, norm_ffn_g[layer]), ffn_w_up[layer], ffn_conv_w[layer], ffn_conv_b[layer],
                     ffn_w_down[layer])
        h = h + f.astype(h.dtype)
    return rmsnorm(h, norm_out_g)
```

```python
import functools
import math

import numpy as np
import jax
import jax.numpy as jnp
from jax import lax
from jax.experimental import pallas as pl
from jax.experimental.pallas import tpu as pltpu

F32 = jnp.float32
BF16 = jnp.bfloat16
I32 = jnp.int32

EPS = 1e-6
NEG = -1e30

D_MODEL = 1024
A_HEADS = 8
A_HEAD_DIM = 64
A_WIDTH = 512
MOBA_BLOCK = 256
MOBA_TOPK = 3
REL_BUCKETS = 32
REL_MAX_DIST = 128
B_HEADS = 4
B_HEAD_DIM = 128
B_WIDTH = 512
MLSTM_CONV = 4
POOL_WINDOWS = (2, 4, 8, 16)
POOL_GROUP_DIM = 128
POOL_WIDTH = 512
R_HEADS = 4
R_QK_DIM = 64
R_V_DIM = 128
R_QK_WIDTH = 256
R_V_WIDTH = 512
ROPE_BASE = 10000.0
D_FF = 2816
FFN_CONV = 3

CHUNK = 256
HALO = 8
POOL_HALO = 16
FF_TILE = 256
PROJ_ROWS = 512
FFN_ROWS = 512
VMEM_LIMIT = 56 * 1024 * 1024

_NT = (((1,), (1,)), ((), ()))


def _dot(a, b):
    return jnp.dot(a, b, preferred_element_type=F32)


def _dot_nt(a, b):
    return lax.dot_general(a, b, _NT, preferred_element_type=F32)


def _split3(a):
    a1 = a.astype(BF16)
    r = a - a1.astype(F32)
    a2 = r.astype(BF16)
    a3 = (r - a2.astype(F32)).astype(BF16)
    return a1, a2, a3


def _rms(x, g):
    return x * lax.rsqrt(jnp.mean(x * x, axis=-1, keepdims=True) + EPS) * g


def _resident(shape):
    n = len(shape)
    return pl.BlockSpec(shape, lambda *_: (0,) * n)


def _bucket_thresholds():
    max_exact = REL_BUCKETS // 2
    d = np.arange(1, 4 * REL_MAX_DIST, dtype=np.float32)
    large = max_exact + (np.log(d / np.float32(max_exact)) / np.float32(math.log(REL_MAX_DIST / max_exact))
                         * np.float32(REL_BUCKETS - max_exact)).astype(np.int32)
    large = np.minimum(large, REL_BUCKETS - 1)
    return [int(d[np.argmax(large >= b)]) for b in range(max_exact + 1, REL_BUCKETS)]


def _bias_kernel(tab_ref, o_ref, *, thresholds):
    h = pl.program_id(0)
    row = lax.broadcasted_iota(I32, (MOBA_BLOCK, MOBA_BLOCK), 0)
    col = lax.broadcasted_iota(I32, (MOBA_BLOCK, MOBA_BLOCK), 1)
    max_exact = REL_BUCKETS // 2
    for t, off in enumerate((0, MOBA_BLOCK)):
        dist = col - row + off
        d = jnp.maximum(dist, 0)
        large = jnp.full(d.shape, max_exact, I32)
        for thr in thresholds:
            large = large + jnp.where(d >= thr, 1, 0)
        bucket = jnp.where(d < max_exact, d, large)
        val = jnp.zeros(d.shape, F32)
        for i in range(REL_BUCKETS):
            val = jnp.where(bucket == i, tab_ref[h, i], val)
        if off == 0:
            val = jnp.where(dist >= 0, val, NEG)
        o_ref[0, t] = val


def _bias_tables(rel_bias):
    tab = rel_bias.astype(F32).T
    return pl.pallas_call(
        functools.partial(_bias_kernel, thresholds=_bucket_thresholds()),
        grid=(A_HEADS,),
        in_specs=[pl.BlockSpec(memory_space=pltpu.SMEM)],
        out_specs=pl.BlockSpec((1, 2, MOBA_BLOCK, MOBA_BLOCK), lambda h: (h, 0, 0, 0)),
        out_shape=jax.ShapeDtypeStruct((A_HEADS, 2, MOBA_BLOCK, MOBA_BLOCK), F32),
        name="rel_bias_tiles",
    )(tab)


def _norm_proj_kernel(h_ref, g_ref, *refs, n_out, col_tile):
    w_refs, o_refs = refs[:n_out], refs[n_out:]
    y = _rms(h_ref[...], g_ref[...]).astype(BF16)
    for w_ref, o_ref in zip(w_refs, o_refs):
        n = w_ref.shape[1]
        for c0 in range(0, n, col_tile):
            c1 = min(c0 + col_tile, n)
            o_ref[:, c0:c1] = _dot(y, w_ref[:, c0:c1]).astype(o_ref.dtype)


def _norm_proj(h2d, g, weights, out_dtypes):
    m, d = h2d.shape
    tm = min(PROJ_ROWS, m)
    n_out = len(weights)
    return pl.pallas_call(
        functools.partial(_norm_proj_kernel, n_out=n_out, col_tile=512),
        grid=(m // tm,),
        in_specs=[pl.BlockSpec((tm, d), lambda i: (i, 0)), _resident((1, d))]
                 + [_resident(w.shape) for w in weights],
        out_specs=[pl.BlockSpec((tm, w.shape[1]), lambda i: (i, 0)) for w in weights],
        out_shape=[jax.ShapeDtypeStruct((m, w.shape[1]), dt) for w, dt in zip(weights, out_dtypes)],
        compiler_params=pltpu.CompilerParams(dimension_semantics=("parallel",), vmem_limit_bytes=VMEM_LIMIT),
        name="norm_in_proj",
    )(h2d, g.reshape(1, d).astype(F32), *weights)


def _moba_kernel(far_ref, q_ref, k_ref, v_ref, bias_ref, o_ref, kmean_ref, vt_ref, addm_ref, *, nb):
    blk = MOBA_BLOCK
    hp = pl.program_id(1)
    qi = pl.program_id(2)

    @pl.when(qi == 0)
    def _():
        for j in range(nb):
            kb = k_ref[0, j * blk:(j + 1) * blk, :].astype(F32)
            kmean_ref[j:j + 1, :] = jnp.sum(kb, axis=0, keepdims=True) * (1.0 / blk)
            vt_ref[j] = v_ref[0, j * blk:(j + 1) * blk, :].astype(F32).T.astype(BF16)

    q = q_ref[0]
    lane = lax.broadcasted_iota(I32, (1, 2 * A_HEAD_DIM), 1)
    rowj = lax.broadcasted_iota(I32, (nb, blk), 0)
    scale = A_HEAD_DIM ** -0.5
    outs = []
    for hh in range(2):
        in_head = (lane >= hh * A_HEAD_DIM) & (lane < (hh + 1) * A_HEAD_DIM)
        qm = jnp.where(in_head, q, jnp.zeros_like(q)) * scale
        km = jnp.where(in_head, kmean_ref[...], 0.0)
        gate = sum(_dot_nt(part, qm) for part in _split3(km))
        gate = jnp.where(rowj < qi, gate, -jnp.inf)
        ranks = []
        for j in range(nb):
            gj = gate[j:j + 1, :]
            ahead = jnp.where(gate > gj, 1.0, 0.0)
            if j > 0:
                ahead = ahead + jnp.where(rowj < j, jnp.where(gate == gj, 1.0, 0.0), 0.0)
            ranks.append(jnp.sum(ahead, axis=0, keepdims=True))
        rank = jnp.concatenate(ranks, axis=0)
        chosen_bias = jnp.where(rowj == qi - 1, 0.0, far_ref[hp * 2 + hh])
        addm_ref[hh] = jnp.where(rowj < qi, jnp.where(rank < float(MOBA_TOPK), chosen_bias, NEG), NEG)

        def visit(j, carry, bias_tile, add_row):
            kb = k_ref[0, pl.ds(pl.multiple_of(j * blk, blk), blk), :]
            s = _dot_nt(kb, qm)
            if add_row:
                s = s + addm_ref[hh, pl.ds(j, 1), :]
            if bias_tile is not None:
                s = s + bias_tile
            vt = vt_ref[j, hh * A_HEAD_DIM:(hh + 1) * A_HEAD_DIM, :]
            s_max = jnp.max(s, axis=0, keepdims=True)
            if carry is None:
                m_new = s_max
                p = jnp.exp(s - m_new)
                return m_new, jnp.sum(p, axis=0, keepdims=True), _dot(vt, p.astype(BF16))
            m, l, acc = carry
            m_new = jnp.maximum(m, s_max)
            alpha = jnp.exp(m - m_new)
            p = jnp.exp(s - m_new)
            return (m_new, alpha * l + jnp.sum(p, axis=0, keepdims=True),
                    alpha * acc + _dot(vt, p.astype(BF16)))

        carry = visit(qi, None, bias_ref[hh, 0], False)
        carry = lax.cond(qi >= 1, lambda c: visit(qi - 1, c, bias_ref[hh, 1], True), lambda c: c, carry)
        carry = lax.fori_loop(0, jnp.maximum(qi - 1, 0), lambda j, c: visit(j, c, None, True), carry)
        _, l, acc = carry
        outs.append(acc / l)
    o_ref[0] = jnp.concatenate(outs, axis=0).T.astype(o_ref.dtype)


def _moba(zqkv, bias_tiles, far_bias):
    bsz, s, _ = zqkv.shape
    nb = s // MOBA_BLOCK
    n_pairs = A_HEADS // 2
    pair_w = 2 * A_HEAD_DIM
    return pl.pallas_call(
        functools.partial(_moba_kernel, nb=nb),
        grid=(bsz, n_pairs, nb),
        in_specs=[pl.BlockSpec(memory_space=pltpu.SMEM),
                  pl.BlockSpec((1, MOBA_BLOCK, pair_w), lambda b, hp, qi: (b, qi, hp)),
                  pl.BlockSpec((1, s, pair_w), lambda b, hp, qi: (b, 0, n_pairs + hp)),
                  pl.BlockSpec((1, s, pair_w), lambda b, hp, qi: (b, 0, 2 * n_pairs + hp)),
                  pl.BlockSpec((2, 2, MOBA_BLOCK, MOBA_BLOCK), lambda b, hp, qi: (hp, 0, 0, 0))],
        out_specs=pl.BlockSpec((1, MOBA_BLOCK, pair_w), lambda b, hp, qi: (b, qi, hp)),
        out_shape=jax.ShapeDtypeStruct((bsz, s, A_WIDTH), BF16),
        scratch_shapes=[pltpu.VMEM((nb, pair_w), F32),
                        pltpu.VMEM((nb, pair_w, MOBA_BLOCK), BF16),
                        pltpu.VMEM((2, nb, MOBA_BLOCK), F32)],
        compiler_params=pltpu.CompilerParams(dimension_semantics=("parallel", "parallel", "arbitrary"),
                                             vmem_limit_bytes=VMEM_LIMIT),
        name="moba_attention",
    )(far_bias, zqkv, zqkv, zqkv, bias_tiles)


def _causal_conv(x_ext, w_ref, taps):
    y = w_ref[taps - 1:taps, :] * x_ext
    for back in range(1, taps):
        y = y + w_ref[taps - 1 - back:taps - back, :] * pltpu.roll(x_ext, back, 0)
    return y[HALO:, :]


def _mlstm_kernel(zb_ref, zg_ref, cw_ref, cb_ref, gb_ref, ng_ref, o_ref, hist_ref, state_ref, m_ref):
    L = CHUNK
    dh = B_HEAD_DIM

    @pl.when(pl.program_id(1) == 0)
    def _():
        hist_ref[...] = jnp.zeros_like(hist_ref)
        state_ref[...] = jnp.zeros_like(state_ref)
        m_ref[...] = jnp.zeros_like(m_ref)

    x = zb_ref[0, :, 0:2 * B_WIDTH].astype(F32)
    x_ext = jnp.concatenate([hist_ref[...], x], axis=0)
    hist_ref[...] = x[L - HALO:, :]
    y = _causal_conv(x_ext, cw_ref, MLSTM_CONV) + cb_ref[...]
    y = y * jax.nn.sigmoid(y)
    q_all = y[:, :B_WIDTH]
    k_all = y[:, B_WIDTH:] * (dh ** -0.5)

    g = zg_ref[0] + gb_ref[...]
    lane = lax.broadcasted_iota(I32, (1, 128), 1)
    log_f = jnp.minimum(g, 0.0) - jnp.log(1.0 + jnp.exp(-jnp.abs(g)))
    gates = jnp.where(lane < B_HEADS, g, log_f)
    row = lax.broadcasted_iota(I32, (L, L), 0)
    col = lax.broadcasted_iota(I32, (L, L), 1)
    causal = row >= col
    tril = jnp.where(causal, 1.0, 0.0).astype(BF16)
    triu = jnp.where(row <= col, 1.0, 0.0).astype(BF16)
    csum_col = sum(_dot(tril, part) for part in _split3(gates))
    gates_t = gates.T
    csum_row = sum(_dot(part, triu) for part in _split3(gates_t[0:16, :]))
    unit = jnp.where(lax.broadcasted_iota(I32, (L, dh), 1) == 0, 1.0, 0.0).astype(BF16)

    for h in range(B_HEADS):
        sl = slice(h * dh, (h + 1) * dh)
        b_col = csum_col[:, B_HEADS + h:B_HEADS + h + 1]
        b_row = csum_row[B_HEADS + h:B_HEADS + h + 1, :]
        i_row = gates_t[h:h + 1, :]
        m_prev = m_ref[h]
        log_inter = b_col + m_prev
        log_intra = jnp.where(causal, b_col - b_row + i_row, -jnp.inf)
        m_t = jnp.maximum(log_inter, jnp.max(log_intra, axis=1, keepdims=True))
        w_inter = jnp.exp(log_inter - m_t)
        qh = q_all[:, sl].astype(BF16)
        kh = k_all[:, sl]
        qk = _dot_nt(qh, kh.astype(BF16)) * jnp.exp(log_intra - m_t)
        v_aug = jnp.concatenate([zb_ref[0, :, 2 * B_WIDTH + h * dh:2 * B_WIDTH + (h + 1) * dh], unit], axis=1)
        state = state_ref[h]
        nd = w_inter * _dot(qh, state.astype(BF16)) + _dot(qk.astype(BF16), v_aug)
        den = nd[:, dh:dh + 1]
        hout = nd[:, :dh] / jnp.maximum(jnp.abs(den), jnp.exp(-m_t))

        b_last = b_row[:, L - 1:L]
        log_s = b_last - b_row + i_row
        m_new = jnp.maximum(b_last + m_prev, jnp.max(log_s, axis=1, keepdims=True))
        w_prev = jnp.exp(b_last + m_prev - m_new)
        w_s = jnp.exp(log_s - m_new)
        state_ref[h] = w_prev * state + _dot((kh.T * w_s).astype(BF16), v_aug)
        m_ref[h] = m_new

        hn = _rms(hout, ng_ref[:, sl])
        og = jax.nn.sigmoid(zb_ref[0, :, 3 * B_WIDTH + h * dh:3 * B_WIDTH + (h + 1) * dh].astype(F32))
        o_ref[0, :, sl] = (hn * og).astype(o_ref.dtype)


def _mlstm(zb, zg, conv_w, conv_b, b_i, b_f, norm_g):
    bsz, s, _ = zb.shape
    gate_bias = jnp.zeros((1, 128), F32).at[0, :B_HEADS].set(b_i.astype(F32)).at[0, B_HEADS:2 * B_HEADS].set(
        b_f.astype(F32))
    return pl.pallas_call(
        _mlstm_kernel,
        grid=(bsz, s // CHUNK),
        in_specs=[pl.BlockSpec((1, CHUNK, 4 * B_WIDTH), lambda b, c: (b, c, 0)),
                  pl.BlockSpec((1, CHUNK, 128), lambda b, c: (b, c, 0)),
                  _resident((MLSTM_CONV, 2 * B_WIDTH)), _resident((1, 2 * B_WIDTH)),
                  _resident((1, 128)), _resident((1, B_WIDTH))],
        out_specs=pl.BlockSpec((1, CHUNK, B_WIDTH), lambda b, c: (b, c, 0)),
        out_shape=jax.ShapeDtypeStruct((bsz, s, B_WIDTH), BF16),
        scratch_shapes=[pltpu.VMEM((HALO, 2 * B_WIDTH), F32),
                        pltpu.VMEM((B_HEADS, B_HEAD_DIM, 2 * B_HEAD_DIM), F32),
                        pltpu.VMEM((B_HEADS, 1, 1), F32)],
        compiler_params=pltpu.CompilerParams(dimension_semantics=("parallel", "arbitrary"),
                                             vmem_limit_bytes=VMEM_LIMIT),
        name="mlstm",
    )(zb, zg.reshape(bsz, s, 128), conv_w.astype(F32), conv_b.reshape(1, -1).astype(F32), gate_bias,
      norm_g.reshape(1, -1).astype(F32))


def _pool_ret_kernel(z_ref, cos_ref, sin_ref, swap_ref, dq_ref, dk_ref, dc_ref, dm_ref, bd_ref, pw_ref, ps_ref,
                     rg_ref, o_ref, hist_ref, state_ref):
    L = CHUNK
    c = pl.program_id(1)

    @pl.when(c == 0)
    def _():
        hist_ref[...] = jnp.zeros_like(hist_ref)
        state_ref[...] = jnp.zeros_like(state_ref)

    u = z_ref[0, :, 0:POOL_WIDTH].astype(F32)
    u_ext = jnp.concatenate([hist_ref[...], u], axis=0)
    hist_ref[...] = u[L - POOL_HALO:, :]
    pos1 = (c * L + lax.broadcasted_iota(I32, (L, 1), 0) + 1).astype(F32)
    gd = POOL_GROUP_DIM
    for g, w in enumerate(POOL_WINDOWS):
        sl = slice(g * gd, (g + 1) * gd)
        win = u_ext[:, sl]
        span = 1
        while span < w:
            win = win + pltpu.roll(win, span, 0)
            span *= 2
        pooled = win[POOL_HALO:, :] / jnp.minimum(pos1, float(w)) - u[:, sl]
        o_ref[0, :, sl] = (_dot(pooled.astype(BF16), pw_ref[g]) * ps_ref[:, sl]).astype(o_ref.dtype)

    rq = z_ref[0, :, POOL_WIDTH:POOL_WIDTH + R_QK_WIDTH]
    rk = z_ref[0, :, POOL_WIDTH + R_QK_WIDTH:POOL_WIDTH + 2 * R_QK_WIDTH]
    v0 = POOL_WIDTH + 2 * R_QK_WIDTH
    rv = z_ref[0, :, v0:v0 + R_V_WIDTH]
    cosv, sinv = cos_ref[...], sin_ref[...]
    q_rot = rq.astype(F32) * cosv + _dot(rq, swap_ref[...]) * sinv
    k_rot = (rk.astype(F32) * cosv + _dot(rk, swap_ref[...]) * sinv) * (R_QK_DIM ** -0.5)
    qb = q_rot.astype(BF16)
    kb = k_rot.astype(BF16)
    inter = _dot((q_rot * dq_ref[...]).astype(BF16), state_ref[...].astype(BF16))
    lane = lax.broadcasted_iota(I32, (1, R_QK_WIDTH), 1)
    g0 = v0 + R_V_WIDTH
    for h in range(R_HEADS):
        in_head = (lane >= h * R_QK_DIM) & (lane < (h + 1) * R_QK_DIM)
        sc = _dot_nt(jnp.where(in_head, qb, jnp.zeros_like(qb)), kb) * dm_ref[h]
        vs = slice(h * R_V_DIM, (h + 1) * R_V_DIM)
        y = _dot(sc.astype(BF16), rv[:, vs]) + inter[:, vs]
        gate = z_ref[0, :, g0 + h * R_V_DIM:g0 + (h + 1) * R_V_DIM].astype(F32)
        gate = gate * jax.nn.sigmoid(gate)
        o_ref[0, :, POOL_WIDTH + h * R_V_DIM:POOL_WIDTH + (h + 1) * R_V_DIM] = (
            _rms(y, rg_ref[:, vs]) * gate).astype(o_ref.dtype)
    upd = _dot((k_rot * dk_ref[...]).T.astype(BF16), rv)
    state_ref[...] = dc_ref[...] * state_ref[...] + upd * bd_ref[...]


def _retention_tables(s):
    L = CHUNK
    half = R_QK_DIM // 2
    inv = ROPE_BASE ** (-jnp.arange(half, dtype=F32) / half)
    ang = jnp.arange(s, dtype=F32)[:, None] * inv[None, :]
    cos, sin = jnp.cos(ang), jnp.sin(ang)
    cos_t = jnp.tile(jnp.concatenate([cos, cos], axis=-1), (1, R_HEADS))
    sin_t = jnp.tile(jnp.concatenate([-sin, sin], axis=-1), (1, R_HEADS))
    j = np.arange(R_QK_WIDTH)
    src = (j // R_QK_DIM) * R_QK_DIM + (j % R_QK_DIM + half) % R_QK_DIM
    swap = np.zeros((R_QK_WIDTH, R_QK_WIDTH), np.float32)
    swap[src, j] = 1.0
    log_g = jnp.log(1.0 - jnp.exp2(-5.0 - jnp.arange(R_HEADS, dtype=F32)))
    t = jnp.arange(L, dtype=F32)
    rel = t[:, None] - t[None, :]
    decay_intra = jnp.where(rel >= 0, jnp.exp(jnp.maximum(rel, 0.0) * log_g[:, None, None]), 0.0)
    decay_q = jnp.exp((t + 1.0) * log_g[:, None])
    decay_k = jnp.exp((L - 1.0 - t) * log_g[:, None])
    decay_chunk = jnp.exp(L * log_g)
    dq = jnp.repeat(decay_q.T, R_QK_DIM, axis=1)
    dk = jnp.repeat(decay_k.T, R_QK_DIM, axis=1)
    dc = jnp.repeat(decay_chunk, R_QK_DIM)[:, None]
    bd = (np.arange(R_QK_WIDTH)[:, None] // R_QK_DIM == np.arange(R_V_WIDTH)[None, :] // R_V_DIM)
    return cos_t, sin_t, jnp.asarray(swap, BF16), dq, dk, dc, decay_intra, jnp.asarray(bd, F32)


def _pool_ret(z, pool_w, pool_scale, ret_g):
    bsz, s, width = z.shape
    cos_t, sin_t, swap, dq, dk, dc, dm, bd = _retention_tables(s)
    return pl.pallas_call(
        _pool_ret_kernel,
        grid=(bsz, s // CHUNK),
        in_specs=[pl.BlockSpec((1, CHUNK, width), lambda b, c: (b, c, 0)),
                  pl.BlockSpec((CHUNK, R_QK_WIDTH), lambda b, c: (c, 0)),
                  pl.BlockSpec((CHUNK, R_QK_WIDTH), lambda b, c: (c, 0)),
                  _resident(swap.shape), _resident(dq.shape), _resident(dk.shape), _resident(dc.shape),
                  _resident(dm.shape), _resident(bd.shape), _resident(pool_w.shape),
                  _resident((1, POOL_WIDTH)), _resident((1, R_V_WIDTH))],
        out_specs=pl.BlockSpec((1, CHUNK, POOL_WIDTH + R_V_WIDTH), lambda b, c: (b, c, 0)),
        out_shape=jax.ShapeDtypeStruct((bsz, s, POOL_WIDTH + R_V_WIDTH), BF16),
        scratch_shapes=[pltpu.VMEM((POOL_HALO, POOL_WIDTH), F32),
                        pltpu.VMEM((R_QK_WIDTH, R_V_WIDTH), F32)],
        compiler_params=pltpu.CompilerParams(dimension_semantics=("parallel", "arbitrary"),
                                             vmem_limit_bytes=VMEM_LIMIT),
        name="pool_retention",
    )(z, cos_t, sin_t, swap, dq, dk, dc, dm, bd, pool_w.astype(BF16), pool_scale.reshape(1, -1).astype(F32),
      ret_g.reshape(1, -1).astype(F32))


def _ffn_kernel(h_ref, a1_ref, a2_ref, wo1_ref, wo2_ref, g_ref, wup_ref, cw_ref, cb_ref, wd_ref, gout_ref, o_ref,
                hist_ref, acc_ref, y_ref, *, n_tiles, final_norm):
    tm = h_ref.shape[1]

    @pl.when(pl.program_id(1) == 0)
    def _():
        hist_ref[...] = jnp.zeros_like(hist_ref)

    h1 = h_ref[0] + _dot(a1_ref[0], wo1_ref[...]) + _dot(a2_ref[0], wo2_ref[...])
    y_ref[...] = _rms(h1, g_ref[...]).astype(BF16)
    acc_ref[...] = h1

    def conv(u, idx):
        x_ext = jnp.concatenate([hist_ref[idx], u], axis=0)
        hist_ref[idx] = u[tm - HALO:, :]
        return _causal_conv(x_ext, cw_ref.at[idx], FFN_CONV) + cb_ref[idx]

    def tile(c, _):
        y = y_ref[...]
        gate = conv(_dot(y, wup_ref[c]), c)
        val = conv(_dot(y, wup_ref[n_tiles + c]), n_tiles + c)
        act = (gate * jax.nn.sigmoid(gate) * val).astype(BF16)
        acc_ref[...] += _dot(act, wd_ref[c])
        return 0

    lax.fori_loop(0, n_tiles, tile, 0)
    out = acc_ref[...]
    if final_norm:
        out = _rms(out, gout_ref[...])
    o_ref[0] = out


def _ffn(h, a1, a2, col1, col2, w_out, norm_g, w_up, conv_w, conv_b, w_down, out_g, final_norm):
    bsz, s, d = h.shape
    tm = min(FFN_ROWS, s)
    half = w_out.shape[0] // 2
    n_tiles = D_FF // FF_TILE
    wo = w_out.astype(BF16)
    wup = w_up.astype(BF16).reshape(d, 2 * n_tiles, FF_TILE).transpose(1, 0, 2)
    cw = jnp.zeros((2 * n_tiles, HALO, FF_TILE), F32).at[:, :FFN_CONV, :].set(
        conv_w.astype(F32).reshape(FFN_CONV, 2 * n_tiles, FF_TILE).transpose(1, 0, 2))
    cb = conv_b.astype(F32).reshape(2 * n_tiles, 1, FF_TILE)
    wd = w_down.astype(BF16).reshape(n_tiles, FF_TILE, d)
    return pl.pallas_call(
        functools.partial(_ffn_kernel, n_tiles=n_tiles, final_norm=final_norm),
        grid=(bsz, s // tm),
        in_specs=[pl.BlockSpec((1, tm, d), lambda b, t: (b, t, 0)),
                  pl.BlockSpec((1, tm, half), lambda b, t: (b, t, col1)),
                  pl.BlockSpec((1, tm, half), lambda b, t: (b, t, col2)),
                  _resident((half, d)), _resident((half, d)), _resident((1, d)),
                  _resident(wup.shape), _resident(cw.shape), _resident(cb.shape), _resident(wd.shape),
                  _resident((1, d))],
        out_specs=pl.BlockSpec((1, tm, d), lambda b, t: (b, t, 0)),
        out_shape=jax.ShapeDtypeStruct((bsz, s, d), F32),
        scratch_shapes=[pltpu.VMEM((2 * n_tiles, HALO, FF_TILE), F32),
                        pltpu.VMEM((tm, d), F32),
                        pltpu.VMEM((tm, d), BF16)],
        compiler_params=pltpu.CompilerParams(dimension_semantics=("parallel", "arbitrary"),
                                             vmem_limit_bytes=VMEM_LIMIT),
        name="out_proj_ffn",
    )(h, a1, a2, wo[:half], wo[half:], norm_g.reshape(1, d).astype(F32), wup, cw, cb, wd,
      out_g.reshape(1, d).astype(F32))


def kernel(x, rel_bias, norm_mix_g, norm_ffn_g, norm_out_g, w_in_ab, w_out_ab, mlstm_conv_w, mlstm_conv_b, mlstm_b_i,
           mlstm_b_f, mlstm_norm_g, w_in_cd, w_out_cd, pool_w, pool_scale, ret_norm_g, ffn_w_up, ffn_conv_w,
           ffn_conv_b, ffn_w_down):
    bsz, s, d = x.shape
    m = bsz * s

    w0 = w_in_ab[0].astype(BF16)
    n_attn = 3 * A_WIDTH
    n_main = n_attn + 4 * B_WIDTH
    w_gate = jnp.zeros((d, 128), BF16).at[:, :2 * B_HEADS].set(w0[:, n_main:])
    zqkv, zb, zg = _norm_proj(x.reshape(m, d), norm_mix_g[0], [w0[:, :n_attn], w0[:, n_attn:n_main], w_gate],
                              [BF16, BF16, F32])
    bias_tiles = _bias_tables(rel_bias)
    far_bias = rel_bias[REL_BUCKETS - 1].astype(F32)
    ya = _moba(zqkv.reshape(bsz, s, n_attn), bias_tiles, far_bias)
    yb = _mlstm(zb.reshape(bsz, s, 4 * B_WIDTH), zg, mlstm_conv_w[0], mlstm_conv_b[0], mlstm_b_i[0], mlstm_b_f[0],
                mlstm_norm_g[0])
    h = _ffn(x, ya, yb, 0, 0, w_out_ab[0], norm_ffn_g[0], ffn_w_up[0], ffn_conv_w[0], ffn_conv_b[0], ffn_w_down[0],
             norm_out_g, False)

    (z1,) = _norm_proj(h.reshape(m, d), norm_mix_g[1], [w_in_cd[0].astype(BF16)], [BF16])
    cat = _pool_ret(z1.reshape(bsz, s, -1), pool_w[0], pool_scale[0], ret_norm_g[0])
    return _ffn(h, cat, cat, 0, 1, w_out_cd[0], norm_ffn_g[1], ffn_w_up[1], ffn_conv_w[1], ffn_conv_b[1],
                ffn_w_down[1], norm_out_g, True)
```

```python
import functools
import math

import numpy as np
import jax
import jax.numpy as jnp
from jax import lax
from jax.experimental import pallas as pl
from jax.experimental.pallas import tpu as pltpu

F32 = jnp.float32
BF16 = jnp.bfloat16
I32 = jnp.int32

EPS = 1e-6
NEG = -1e30

D_MODEL = 1024
A_HEADS = 8
A_HEAD_DIM = 64
A_WIDTH = 512
MOBA_BLOCK = 256
MOBA_TOPK = 3
REL_BUCKETS = 32
REL_MAX_DIST = 128
B_HEADS = 4
B_HEAD_DIM = 128
B_WIDTH = 512
MLSTM_CONV = 4
POOL_WINDOWS = (2, 4, 8, 16)
POOL_GROUP_DIM = 128
POOL_WIDTH = 512
R_HEADS = 4
R_QK_DIM = 64
R_V_DIM = 128
R_QK_WIDTH = 256
R_V_WIDTH = 512
ROPE_BASE = 10000.0
D_FF = 2816
FFN_CONV = 3

CHUNK = 256
HALO = 8
POOL_HALO = 16
FF_TILE = 256
PROJ_ROWS = 512
FFN_ROWS = 512
VMEM_LIMIT = 56 * 1024 * 1024

_NT = (((1,), (1,)), ((), ()))


def _dot(a, b):
    return jnp.dot(a, b, preferred_element_type=F32)


def _dot_nt(a, b):
    return lax.dot_general(a, b, _NT, preferred_element_type=F32)


def _split3(a):
    a1 = a.astype(BF16)
    r = a - a1.astype(F32)
    a2 = r.astype(BF16)
    a3 = (r - a2.astype(F32)).astype(BF16)
    return a1, a2, a3


def _rms(x, g):
    return x * lax.rsqrt(jnp.mean(x * x, axis=-1, keepdims=True) + EPS) * g


def _resident(shape):
    n = len(shape)
    return pl.BlockSpec(shape, lambda *_: (0,) * n)


def _bucket_thresholds():
    max_exact = REL_BUCKETS // 2
    d = np.arange(1, 4 * REL_MAX_DIST, dtype=np.float32)
    large = max_exact + (np.log(d / np.float32(max_exact)) / np.float32(math.log(REL_MAX_DIST / max_exact))
                         * np.float32(REL_BUCKETS - max_exact)).astype(np.int32)
    large = np.minimum(large, REL_BUCKETS - 1)
    return [int(d[np.argmax(large >= b)]) for b in range(max_exact + 1, REL_BUCKETS)]


def _bias_kernel(tab_ref, o_ref, *, thresholds):
    h = pl.program_id(0)
    row = lax.broadcasted_iota(I32, (MOBA_BLOCK, MOBA_BLOCK), 0)
    col = lax.broadcasted_iota(I32, (MOBA_BLOCK, MOBA_BLOCK), 1)
    max_exact = REL_BUCKETS // 2
    for t, off in enumerate((0, MOBA_BLOCK)):
        dist = col - row + off
        d = jnp.maximum(dist, 0)
        large = jnp.full(d.shape, max_exact, I32)
        for thr in thresholds:
            large = large + jnp.where(d >= thr, 1, 0)
        bucket = jnp.where(d < max_exact, d, large)
        val = jnp.zeros(d.shape, F32)
        for i in range(REL_BUCKETS):
            val = jnp.where(bucket == i, tab_ref[h, i], val)
        if off == 0:
            val = jnp.where(dist >= 0, val, NEG)
        o_ref[0, t] = val


def _bias_tables(rel_bias):
    tab = rel_bias.astype(F32).T
    return pl.pallas_call(
        functools.partial(_bias_kernel, thresholds=_bucket_thresholds()),
        grid=(A_HEADS,),
        in_specs=[pl.BlockSpec(memory_space=pltpu.SMEM)],
        out_specs=pl.BlockSpec((1, 2, MOBA_BLOCK, MOBA_BLOCK), lambda h: (h, 0, 0, 0)),
        out_shape=jax.ShapeDtypeStruct((A_HEADS, 2, MOBA_BLOCK, MOBA_BLOCK), F32),
        name="rel_bias_tiles",
    )(tab)


def _norm_proj_kernel(h_ref, g_ref, *refs, n_out, col_tile):
    w_refs, o_refs = refs[:n_out], refs[n_out:]
    y = _rms(h_ref[...], g_ref[...]).astype(BF16)
    for w_ref, o_ref in zip(w_refs, o_refs):
        n = w_ref.shape[1]
        for c0 in range(0, n, col_tile):
            c1 = min(c0 + col_tile, n)
            o_ref[:, c0:c1] = _dot(y, w_ref[:, c0:c1]).astype(o_ref.dtype)


def _norm_proj(h2d, g, weights, out_dtypes):
    m, d = h2d.shape
    tm = min(PROJ_ROWS, m)
    n_out = len(weights)
    return pl.pallas_call(
        functools.partial(_norm_proj_kernel, n_out=n_out, col_tile=512),
        grid=(m // tm,),
        in_specs=[pl.BlockSpec((tm, d), lambda i: (i, 0)), _resident((1, d))]
                 + [_resident(w.shape) for w in weights],
        out_specs=[pl.BlockSpec((tm, w.shape[1]), lambda i: (i, 0)) for w in weights],
        out_shape=[jax.ShapeDtypeStruct((m, w.shape[1]), dt) for w, dt in zip(weights, out_dtypes)],
        compiler_params=pltpu.CompilerParams(dimension_semantics=("parallel",), vmem_limit_bytes=VMEM_LIMIT),
        name="norm_in_proj",
    )(h2d, g.reshape(1, d).astype(F32), *weights)


def _moba_kernel(far_ref, q_ref, k_ref, v_ref, bias_ref, o_ref, kmean_ref, vt_ref, addm_ref, *, nb):
    blk = MOBA_BLOCK
    hd = A_HEAD_DIM
    hp = pl.program_id(1)
    qi = pl.program_id(2)

    @pl.when(qi == 0)
    def _():
        for j in range(nb):
            kb = k_ref[0, j * blk:(j + 1) * blk, :].astype(F32)
            kmean_ref[j:j + 1, :] = jnp.sum(kb, axis=0, keepdims=True) * (1.0 / blk)
            vt_ref[j] = v_ref[0, j * blk:(j + 1) * blk, :].astype(F32).T.astype(BF16)
        addm_ref[:, nb:, :] = jnp.full((2, HALO, blk), NEG, F32)

    q = q_ref[0]
    lane = lax.broadcasted_iota(I32, (1, 2 * hd), 1)
    rowj = lax.broadcasted_iota(I32, (nb, blk), 0)
    scale = hd ** -0.5
    qms = []
    for hh in range(2):
        in_head = (lane >= hh * hd) & (lane < (hh + 1) * hd)
        qm = jnp.where(in_head, q, jnp.zeros_like(q)) * scale
        qms.append(qm)
        km = jnp.where(in_head, kmean_ref[...], 0.0)
        gate = sum(_dot_nt(part, qm) for part in _split3(km))
        gate = jnp.where(rowj < qi, gate, -jnp.inf)
        ranks = []
        for j in range(nb):
            gj = gate[j:j + 1, :]
            ahead = jnp.where(gate > gj, 1.0, 0.0)
            if j > 0:
                ahead = ahead + jnp.where(rowj < j, jnp.where(gate == gj, 1.0, 0.0), 0.0)
            ranks.append(jnp.sum(ahead, axis=0, keepdims=True))
        rank = jnp.concatenate(ranks, axis=0)
        chosen_bias = jnp.where(rowj == qi - 1, 0.0, far_ref[hp * 2 + hh])
        addm_ref[hh, 0:nb, :] = jnp.where(rowj < qi, jnp.where(rank < float(MOBA_TOPK), chosen_bias, NEG), NEG)

    def k_rows(start, rows):
        return k_ref[0, pl.ds(pl.multiple_of(start, blk), rows), :]

    def softmax_steps(carries, scores, blocks):
        heads = range(2)
        maxes = [functools.reduce(jnp.maximum, [jnp.max(s, axis=0, keepdims=True) for s in scores[hh]])
                 for hh in heads]
        m_new = [maxes[hh] if carries[hh] is None else jnp.maximum(carries[hh][0], maxes[hh]) for hh in heads]
        ps = [[jnp.exp(s - m_new[hh]) for s in scores[hh]] for hh in heads]
        pv = [sum(_dot(vt_ref[j, hh * hd:(hh + 1) * hd, :], p.astype(BF16)) for j, p in zip(blocks, ps[hh]))
              for hh in heads]
        out = []
        for hh in heads:
            l_new = sum(jnp.sum(p, axis=0, keepdims=True) for p in ps[hh])
            acc_new = pv[hh]
            if carries[hh] is not None:
                alpha = jnp.exp(carries[hh][0] - m_new[hh])
                l_new = alpha * carries[hh][1] + l_new
                acc_new = alpha * carries[hh][2] + acc_new
            out.append((m_new[hh], l_new, acc_new))
        return out

    pc = jnp.maximum(qi - 1, 0)
    k_prev, k_own = k_rows(pc * blk, blk), k_rows(qi * blk, blk)
    qk = [(_dot_nt(k_prev, qms[hh]), _dot_nt(k_own, qms[hh])) for hh in range(2)]
    near = [[qk[hh][0] + bias_ref[hh, 1] + addm_ref[hh, pl.ds(pc, 1), :], qk[hh][1] + bias_ref[hh, 0]]
            for hh in range(2)]
    carry = softmax_steps([None, None], near, [pc, qi])

    def far_pair(t, carry):
        j0, j1 = 2 * t, 2 * t + 1
        r1 = jnp.where(j1 <= qi - 2, j1, nb)
        kb = k_rows(j0 * blk, 2 * blk)
        ss = [_dot_nt(kb, qms[hh]) for hh in range(2)]
        far = [[ss[hh][:blk] + addm_ref[hh, pl.ds(j0, 1), :], ss[hh][blk:] + addm_ref[hh, pl.ds(r1, 1), :]]
               for hh in range(2)]
        new = softmax_steps([carry[0:3], carry[3:6]], far, [j0, j1])
        return new[0] + new[1]

    carry = lax.fori_loop(0, jnp.maximum(qi, 1) // 2, far_pair, carry[0] + carry[1])
    outs = [carry[3 * hh + 2] / carry[3 * hh + 1] for hh in range(2)]
    o_ref[0] = jnp.concatenate(outs, axis=0).T.astype(o_ref.dtype)


def _moba(zqkv, bias_tiles, far_bias):
    bsz, s, _ = zqkv.shape
    nb = s // MOBA_BLOCK
    n_pairs = A_HEADS // 2
    pair_w = 2 * A_HEAD_DIM
    return pl.pallas_call(
        functools.partial(_moba_kernel, nb=nb),
        grid=(bsz, n_pairs, nb),
        in_specs=[pl.BlockSpec(memory_space=pltpu.SMEM),
                  pl.BlockSpec((1, MOBA_BLOCK, pair_w), lambda b, hp, qi: (b, qi, hp)),
                  pl.BlockSpec((1, s, pair_w), lambda b, hp, qi: (b, 0, n_pairs + hp)),
                  pl.BlockSpec((1, s, pair_w), lambda b, hp, qi: (b, 0, 2 * n_pairs + hp)),
                  pl.BlockSpec((2, 2, MOBA_BLOCK, MOBA_BLOCK), lambda b, hp, qi: (hp, 0, 0, 0))],
        out_specs=pl.BlockSpec((1, MOBA_BLOCK, pair_w), lambda b, hp, qi: (b, qi, hp)),
        out_shape=jax.ShapeDtypeStruct((bsz, s, A_WIDTH), BF16),
        scratch_shapes=[pltpu.VMEM((nb, pair_w), F32),
                        pltpu.VMEM((nb, pair_w, MOBA_BLOCK), BF16),
                        pltpu.VMEM((2, nb + HALO, MOBA_BLOCK), F32)],
        compiler_params=pltpu.CompilerParams(dimension_semantics=("parallel", "parallel", "arbitrary"),
                                             vmem_limit_bytes=VMEM_LIMIT),
        name="moba_attention",
    )(far_bias, zqkv, zqkv, zqkv, bias_tiles)


def _causal_conv(x_ext, w_ref, taps):
    y = w_ref[taps - 1:taps, :] * x_ext
    for back in range(1, taps):
        y = y + w_ref[taps - 1 - back:taps - back, :] * pltpu.roll(x_ext, back, 0)
    return y[HALO:, :]


def _mlstm_kernel(zb_ref, zg_ref, cw_ref, cb_ref, gb_ref, ng_ref, o_ref, hist_ref, state_ref, m_ref):
    L = CHUNK
    dh = B_HEAD_DIM

    @pl.when(pl.program_id(1) == 0)
    def _():
        hist_ref[...] = jnp.zeros_like(hist_ref)
        state_ref[...] = jnp.zeros_like(state_ref)
        m_ref[...] = jnp.zeros_like(m_ref)

    x = zb_ref[0, :, 0:2 * B_WIDTH].astype(F32)
    x_ext = jnp.concatenate([hist_ref[...], x], axis=0)
    hist_ref[...] = x[L - HALO:, :]
    y = _causal_conv(x_ext, cw_ref, MLSTM_CONV) + cb_ref[...]
    y = y * jax.nn.sigmoid(y)
    q_all = y[:, :B_WIDTH]
    k_all = y[:, B_WIDTH:] * (dh ** -0.5)

    g = zg_ref[0] + gb_ref[...]
    lane = lax.broadcasted_iota(I32, (1, 128), 1)
    log_f = jnp.minimum(g, 0.0) - jnp.log(1.0 + jnp.exp(-jnp.abs(g)))
    gates = jnp.where(lane < B_HEADS, g, log_f)
    row = lax.broadcasted_iota(I32, (L, L), 0)
    col = lax.broadcasted_iota(I32, (L, L), 1)
    causal = row >= col
    tril = jnp.where(causal, 1.0, 0.0).astype(BF16)
    triu = jnp.where(row <= col, 1.0, 0.0).astype(BF16)
    csum_col = sum(_dot(tril, part) for part in _split3(gates))
    gates_t = gates.T
    csum_row = sum(_dot(part, triu) for part in _split3(gates_t[0:16, :]))
    unit = jnp.where(lax.broadcasted_iota(I32, (L, dh), 1) == 0, 1.0, 0.0).astype(BF16)

    for h in range(B_HEADS):
        sl = slice(h * dh, (h + 1) * dh)
        b_col = csum_col[:, B_HEADS + h:B_HEADS + h + 1]
        b_row = csum_row[B_HEADS + h:B_HEADS + h + 1, :]
        i_row = gates_t[h:h + 1, :]
        m_prev = m_ref[h]
        log_inter = b_col + m_prev
        log_intra = jnp.where(causal, b_col - b_row + i_row, -jnp.inf)
        m_t = jnp.maximum(log_inter, jnp.max(log_intra, axis=1, keepdims=True))
        w_inter = jnp.exp(log_inter - m_t)
        qh = q_all[:, sl].astype(BF16)
        kh = k_all[:, sl]
        qk = _dot_nt(qh, kh.astype(BF16)) * jnp.exp(log_intra - m_t)
        v_aug = jnp.concatenate([zb_ref[0, :, 2 * B_WIDTH + h * dh:2 * B_WIDTH + (h + 1) * dh], unit], axis=1)
        state = state_ref[h]
        nd = w_inter * _dot(qh, state.astype(BF16)) + _dot(qk.astype(BF16), v_aug)
        den = nd[:, dh:dh + 1]
        hout = nd[:, :dh] / jnp.maximum(jnp.abs(den), jnp.exp(-m_t))

        b_last = b_row[:, L - 1:L]
        log_s = b_last - b_row + i_row
        m_new = jnp.maximum(b_last + m_prev, jnp.max(log_s, axis=1, keepdims=True))
        w_prev = jnp.exp(b_last + m_prev - m_new)
        w_s = jnp.exp(log_s - m_new)
        state_ref[h] = w_prev * state + _dot((kh.T * w_s).astype(BF16), v_aug)
        m_ref[h] = m_new

        hn = _rms(hout, ng_ref[:, sl])
        og = jax.nn.sigmoid(zb_ref[0, :, 3 * B_WIDTH + h * dh:3 * B_WIDTH + (h + 1) * dh].astype(F32))
        o_ref[0, :, sl] = (hn * og).astype(o_ref.dtype)


def _mlstm(zb, zg, conv_w, conv_b, b_i, b_f, norm_g):
    bsz, s, _ = zb.shape
    gate_bias = jnp.zeros((1, 128), F32).at[0, :B_HEADS].set(b_i.astype(F32)).at[0, B_HEADS:2 * B_HEADS].set(
        b_f.astype(F32))
    return pl.pallas_call(
        _mlstm_kernel,
        grid=(bsz, s // CHUNK),
        in_specs=[pl.BlockSpec((1, CHUNK, 4 * B_WIDTH), lambda b, c: (b, c, 0)),
                  pl.BlockSpec((1, CHUNK, 128), lambda b, c: (b, c, 0)),
                  _resident((MLSTM_CONV, 2 * B_WIDTH)), _resident((1, 2 * B_WIDTH)),
                  _resident((1, 128)), _resident((1, B_WIDTH))],
        out_specs=pl.BlockSpec((1, CHUNK, B_WIDTH), lambda b, c: (b, c, 0)),
        out_shape=jax.ShapeDtypeStruct((bsz, s, B_WIDTH), BF16),
        scratch_shapes=[pltpu.VMEM((HALO, 2 * B_WIDTH), F32),
                        pltpu.VMEM((B_HEADS, B_HEAD_DIM, 2 * B_HEAD_DIM), F32),
                        pltpu.VMEM((B_HEADS, 1, 1), F32)],
        compiler_params=pltpu.CompilerParams(dimension_semantics=("parallel", "arbitrary"),
                                             vmem_limit_bytes=VMEM_LIMIT),
        name="mlstm",
    )(zb, zg.reshape(bsz, s, 128), conv_w.astype(F32), conv_b.reshape(1, -1).astype(F32), gate_bias,
      norm_g.reshape(1, -1).astype(F32))


def _pool_ret_kernel(z_ref, cos_ref, sin_ref, swap_ref, dq_ref, dk_ref, dc_ref, dm_ref, bd_ref, pw_ref, ps_ref,
                     rg_ref, o_ref, hist_ref, state_ref):
    L = CHUNK
    c = pl.program_id(1)

    @pl.when(c == 0)
    def _():
        hist_ref[...] = jnp.zeros_like(hist_ref)
        state_ref[...] = jnp.zeros_like(state_ref)

    u = z_ref[0, :, 0:POOL_WIDTH].astype(F32)
    u_ext = jnp.concatenate([hist_ref[...], u], axis=0)
    hist_ref[...] = u[L - POOL_HALO:, :]
    pos1 = (c * L + lax.broadcasted_iota(I32, (L, 1), 0) + 1).astype(F32)
    gd = POOL_GROUP_DIM
    for g, w in enumerate(POOL_WINDOWS):
        sl = slice(g * gd, (g + 1) * gd)
        win = u_ext[:, sl]
        span = 1
        while span < w:
            win = win + pltpu.roll(win, span, 0)
            span *= 2
        pooled = win[POOL_HALO:, :] / jnp.minimum(pos1, float(w)) - u[:, sl]
        o_ref[0, :, sl] = (_dot(pooled.astype(BF16), pw_ref[g]) * ps_ref[:, sl]).astype(o_ref.dtype)

    rq = z_ref[0, :, POOL_WIDTH:POOL_WIDTH + R_QK_WIDTH]
    rk = z_ref[0, :, POOL_WIDTH + R_QK_WIDTH:POOL_WIDTH + 2 * R_QK_WIDTH]
    v0 = POOL_WIDTH + 2 * R_QK_WIDTH
    rv = z_ref[0, :, v0:v0 + R_V_WIDTH]
    cosv, sinv = cos_ref[...], sin_ref[...]
    q_rot = rq.astype(F32) * cosv + _dot(rq, swap_ref[...]) * sinv
    k_rot = (rk.astype(F32) * cosv + _dot(rk, swap_ref[...]) * sinv) * (R_QK_DIM ** -0.5)
    qb = q_rot.astype(BF16)
    kb = k_rot.astype(BF16)
    inter = _dot((q_rot * dq_ref[...]).astype(BF16), state_ref[...].astype(BF16))
    lane = lax.broadcasted_iota(I32, (1, R_QK_WIDTH), 1)
    g0 = v0 + R_V_WIDTH
    for h in range(R_HEADS):
        in_head = (lane >= h * R_QK_DIM) & (lane < (h + 1) * R_QK_DIM)
        sc = _dot_nt(jnp.where(in_head, qb, jnp.zeros_like(qb)), kb) * dm_ref[h]
        vs = slice(h * R_V_DIM, (h + 1) * R_V_DIM)
        y = _dot(sc.astype(BF16), rv[:, vs]) + inter[:, vs]
        gate = z_ref[0, :, g0 + h * R_V_DIM:g0 + (h + 1) * R_V_DIM].astype(F32)
        gate = gate * jax.nn.sigmoid(gate)
        o_ref[0, :, POOL_WIDTH + h * R_V_DIM:POOL_WIDTH + (h + 1) * R_V_DIM] = (
            _rms(y, rg_ref[:, vs]) * gate).astype(o_ref.dtype)
    upd = _dot((k_rot * dk_ref[...]).T.astype(BF16), rv)
    state_ref[...] = dc_ref[...] * state_ref[...] + upd * bd_ref[...]


def _retention_tables(s):
    L = CHUNK
    half = R_QK_DIM // 2
    inv = ROPE_BASE ** (-jnp.arange(half, dtype=F32) / half)
    ang = jnp.arange(s, dtype=F32)[:, None] * inv[None, :]
    cos, sin = jnp.cos(ang), jnp.sin(ang)
    cos_t = jnp.tile(jnp.concatenate([cos, cos], axis=-1), (1, R_HEADS))
    sin_t = jnp.tile(jnp.concatenate([-sin, sin], axis=-1), (1, R_HEADS))
    j = np.arange(R_QK_WIDTH)
    src = (j // R_QK_DIM) * R_QK_DIM + (j % R_QK_DIM + half) % R_QK_DIM
    swap = np.zeros((R_QK_WIDTH, R_QK_WIDTH), np.float32)
    swap[src, j] = 1.0
    log_g = jnp.log(1.0 - jnp.exp2(-5.0 - jnp.arange(R_HEADS, dtype=F32)))
    t = jnp.arange(L, dtype=F32)
    rel = t[:, None] - t[None, :]
    decay_intra = jnp.where(rel >= 0, jnp.exp(jnp.maximum(rel, 0.0) * log_g[:, None, None]), 0.0)
    decay_q = jnp.exp((t + 1.0) * log_g[:, None])
    decay_k = jnp.exp((L - 1.0 - t) * log_g[:, None])
    decay_chunk = jnp.exp(L * log_g)
    dq = jnp.repeat(decay_q.T, R_QK_DIM, axis=1)
    dk = jnp.repeat(decay_k.T, R_QK_DIM, axis=1)
    dc = jnp.repeat(decay_chunk, R_QK_DIM)[:, None]
    bd = (np.arange(R_QK_WIDTH)[:, None] // R_QK_DIM == np.arange(R_V_WIDTH)[None, :] // R_V_DIM)
    return cos_t, sin_t, jnp.asarray(swap, BF16), dq, dk, dc, decay_intra, jnp.asarray(bd, F32)


def _pool_ret(z, pool_w, pool_scale, ret_g):
    bsz, s, width = z.shape
    cos_t, sin_t, swap, dq, dk, dc, dm, bd = _retention_tables(s)
    return pl.pallas_call(
        _pool_ret_kernel,
        grid=(bsz, s // CHUNK),
        in_specs=[pl.BlockSpec((1, CHUNK, width), lambda b, c: (b, c, 0)),
                  pl.BlockSpec((CHUNK, R_QK_WIDTH), lambda b, c: (c, 0)),
                  pl.BlockSpec((CHUNK, R_QK_WIDTH), lambda b, c: (c, 0)),
                  _resident(swap.shape), _resident(dq.shape), _resident(dk.shape), _resident(dc.shape),
                  _resident(dm.shape), _resident(bd.shape), _resident(pool_w.shape),
                  _resident((1, POOL_WIDTH)), _resident((1, R_V_WIDTH))],
        out_specs=pl.BlockSpec((1, CHUNK, POOL_WIDTH + R_V_WIDTH), lambda b, c: (b, c, 0)),
        out_shape=jax.ShapeDtypeStruct((bsz, s, POOL_WIDTH + R_V_WIDTH), BF16),
        scratch_shapes=[pltpu.VMEM((POOL_HALO, POOL_WIDTH), F32),
                        pltpu.VMEM((R_QK_WIDTH, R_V_WIDTH), F32)],
        compiler_params=pltpu.CompilerParams(dimension_semantics=("parallel", "arbitrary"),
                                             vmem_limit_bytes=VMEM_LIMIT),
        name="pool_retention",
    )(z, cos_t, sin_t, swap, dq, dk, dc, dm, bd, pool_w.astype(BF16), pool_scale.reshape(1, -1).astype(F32),
      ret_g.reshape(1, -1).astype(F32))


def _ffn_kernel(h_ref, a1_ref, a2_ref, wo1_ref, wo2_ref, g_ref, wup_ref, cw_ref, cb_ref, wd_ref, gout_ref, o_ref,
                hist_ref, acc_ref, y_ref, *, n_tiles, final_norm):
    tm = h_ref.shape[1]

    @pl.when(pl.program_id(1) == 0)
    def _():
        hist_ref[...] = jnp.zeros_like(hist_ref)

    h1 = h_ref[0] + _dot(a1_ref[0], wo1_ref[...]) + _dot(a2_ref[0], wo2_ref[...])
    y_ref[...] = _rms(h1, g_ref[...]).astype(BF16)
    acc_ref[...] = h1

    def conv(u, idx):
        x_ext = jnp.concatenate([hist_ref[idx], u], axis=0)
        hist_ref[idx] = u[tm - HALO:, :]
        return _causal_conv(x_ext, cw_ref.at[idx], FFN_CONV) + cb_ref[idx]

    def tile(c, _):
        y = y_ref[...]
        gate = conv(_dot(y, wup_ref[c]), c)
        val = conv(_dot(y, wup_ref[n_tiles + c]), n_tiles + c)
        act = (gate * jax.nn.sigmoid(gate) * val).astype(BF16)
        acc_ref[...] += _dot(act, wd_ref[c])
        return 0

    lax.fori_loop(0, n_tiles, tile, 0)
    out = acc_ref[...]
    if final_norm:
        out = _rms(out, gout_ref[...])
    o_ref[0] = out


def _ffn(h, a1, a2, col1, col2, w_out, norm_g, w_up, conv_w, conv_b, w_down, out_g, final_norm):
    bsz, s, d = h.shape
    tm = min(FFN_ROWS, s)
    half = w_out.shape[0] // 2
    n_tiles = D_FF // FF_TILE
    wo = w_out.astype(BF16)
    wup = w_up.astype(BF16).reshape(d, 2 * n_tiles, FF_TILE).transpose(1, 0, 2)
    cw = jnp.zeros((2 * n_tiles, HALO, FF_TILE), F32).at[:, :FFN_CONV, :].set(
        conv_w.astype(F32).reshape(FFN_CONV, 2 * n_tiles, FF_TILE).transpose(1, 0, 2))
    cb = conv_b.astype(F32).reshape(2 * n_tiles, 1, FF_TILE)
    wd = w_down.astype(BF16).reshape(n_tiles, FF_TILE, d)
    return pl.pallas_call(
        functools.partial(_ffn_kernel, n_tiles=n_tiles, final_norm=final_norm),
        grid=(bsz, s // tm),
        in_specs=[pl.BlockSpec((1, tm, d), lambda b, t: (b, t, 0)),
                  pl.BlockSpec((1, tm, half), lambda b, t: (b, t, col1)),
                  pl.BlockSpec((1, tm, half), lambda b, t: (b, t, col2)),
                  _resident((half, d)), _resident((half, d)), _resident((1, d)),
                  _resident(wup.shape), _resident(cw.shape), _resident(cb.shape), _resident(wd.shape),
                  _resident((1, d))],
        out_specs=pl.BlockSpec((1, tm, d), lambda b, t: (b, t, 0)),
        out_shape=jax.ShapeDtypeStruct((bsz, s, d), F32),
        scratch_shapes=[pltpu.VMEM((2 * n_tiles, HALO, FF_TILE), F32),
                        pltpu.VMEM((tm, d), F32),
                        pltpu.VMEM((tm, d), BF16)],
        compiler_params=pltpu.CompilerParams(dimension_semantics=("parallel", "arbitrary"),
                                             vmem_limit_bytes=VMEM_LIMIT),
        name="out_proj_ffn",
    )(h, a1, a2, wo[:half], wo[half:], norm_g.reshape(1, d).astype(F32), wup, cw, cb, wd,
      out_g.reshape(1, d).astype(F32))


def kernel(x, rel_bias, norm_mix_g, norm_ffn_g, norm_out_g, w_in_ab, w_out_ab, mlstm_conv_w, mlstm_conv_b, mlstm_b_i,
           mlstm_b_f, mlstm_norm_g, w_in_cd, w_out_cd, pool_w, pool_scale, ret_norm_g, ffn_w_up, ffn_conv_w,
           ffn_conv_b, ffn_w_down):
    bsz, s, d = x.shape
    m = bsz * s

    w0 = w_in_ab[0].astype(BF16)
    n_attn = 3 * A_WIDTH
    n_main = n_attn + 4 * B_WIDTH
    w_gate = jnp.zeros((d, 128), BF16).at[:, :2 * B_HEADS].set(w0[:, n_main:])
    zqkv, zb, zg = _norm_proj(x.reshape(m, d), norm_mix_g[0], [w0[:, :n_attn], w0[:, n_attn:n_main], w_gate],
                              [BF16, BF16, F32])
    bias_tiles = _bias_tables(rel_bias)
    far_bias = rel_bias[REL_BUCKETS - 1].astype(F32)
    ya = _moba(zqkv.reshape(bsz, s, n_attn), bias_tiles, far_bias)
    yb = _mlstm(zb.reshape(bsz, s, 4 * B_WIDTH), zg, mlstm_conv_w[0], mlstm_conv_b[0], mlstm_b_i[0], mlstm_b_f[0],
                mlstm_norm_g[0])
    h = _ffn(x, ya, yb, 0, 0, w_out_ab[0], norm_ffn_g[0], ffn_w_up[0], ffn_conv_w[0], ffn_conv_b[0], ffn_w_down[0],
             norm_out_g, False)

    (z1,) = _norm_proj(h.reshape(m, d), norm_mix_g[1], [w_in_cd[0].astype(BF16)], [BF16])
    cat = _pool_ret(z1.reshape(bsz, s, -1), pool_w[0], pool_scale[0], ret_norm_g[0])
    return _ffn(h, cat, cat, 0, 1, w_out_cd[0], norm_ffn_g[1], ffn_w_up[1], ffn_conv_w[1], ffn_conv_b[1],
                ffn_w_down[1], norm_out_g, True)
```

```python
import functools
import math

import numpy as np
import jax
import jax.numpy as jnp
from jax import lax
from jax.experimental import pallas as pl
from jax.experimental.pallas import tpu as pltpu

F32 = jnp.float32
BF16 = jnp.bfloat16
I32 = jnp.int32

EPS = 1e-6
NEG = -1e30

D_MODEL = 1024
A_HEADS = 8
A_HEAD_DIM = 64
A_WIDTH = 512
MOBA_BLOCK = 256
MOBA_TOPK = 3
REL_BUCKETS = 32
REL_MAX_DIST = 128
B_HEADS = 4
B_HEAD_DIM = 128
B_WIDTH = 512
MLSTM_CONV = 4
POOL_WINDOWS = (2, 4, 8, 16)
POOL_GROUP_DIM = 128
POOL_WIDTH = 512
R_HEADS = 4
R_QK_DIM = 64
R_V_DIM = 128
R_QK_WIDTH = 256
R_V_WIDTH = 512
ROPE_BASE = 10000.0
D_FF = 2816
FFN_CONV = 3

CHUNK = 256
HALO = 8
POOL_HALO = 16
FF_TILE = 256
PROJ_ROWS = 512
FFN_ROWS = 512
VMEM_LIMIT = 56 * 1024 * 1024

_NT = (((1,), (1,)), ((), ()))


def _dot(a, b):
    return jnp.dot(a, b, preferred_element_type=F32)


def _dot_nt(a, b):
    return lax.dot_general(a, b, _NT, preferred_element_type=F32)


def _split3(a):
    a1 = a.astype(BF16)
    r = a - a1.astype(F32)
    a2 = r.astype(BF16)
    a3 = (r - a2.astype(F32)).astype(BF16)
    return a1, a2, a3


def _rms(x, g):
    return x * lax.rsqrt(jnp.mean(x * x, axis=-1, keepdims=True) + EPS) * g


def _resident(shape):
    n = len(shape)
    return pl.BlockSpec(shape, lambda *_: (0,) * n, pipeline_mode=pl.Buffered(1))


def _bucket_thresholds():
    max_exact = REL_BUCKETS // 2
    d = np.arange(1, 4 * REL_MAX_DIST, dtype=np.float32)
    large = max_exact + (np.log(d / np.float32(max_exact)) / np.float32(math.log(REL_MAX_DIST / max_exact))
                         * np.float32(REL_BUCKETS - max_exact)).astype(np.int32)
    large = np.minimum(large, REL_BUCKETS - 1)
    return [int(d[np.argmax(large >= b)]) for b in range(max_exact + 1, REL_BUCKETS)]


def _bias_kernel(tab_ref, o_ref, *, thresholds):
    h = pl.program_id(0)
    row = lax.broadcasted_iota(I32, (MOBA_BLOCK, MOBA_BLOCK), 0)
    col = lax.broadcasted_iota(I32, (MOBA_BLOCK, MOBA_BLOCK), 1)
    max_exact = REL_BUCKETS // 2
    for t, off in enumerate((0, MOBA_BLOCK)):
        dist = col - row + off
        d = jnp.maximum(dist, 0)
        large = jnp.full(d.shape, max_exact, I32)
        for thr in thresholds:
            large = large + jnp.where(d >= thr, 1, 0)
        bucket = jnp.where(d < max_exact, d, large)
        val = jnp.zeros(d.shape, F32)
        for i in range(REL_BUCKETS):
            val = jnp.where(bucket == i, tab_ref[h, i], val)
        if off == 0:
            val = jnp.where(dist >= 0, val, NEG)
        o_ref[0, t] = val


def _bias_tables(rel_bias):
    tab = rel_bias.astype(F32).T
    return pl.pallas_call(
        functools.partial(_bias_kernel, thresholds=_bucket_thresholds()),
        grid=(A_HEADS,),
        in_specs=[pl.BlockSpec(memory_space=pltpu.SMEM)],
        out_specs=pl.BlockSpec((1, 2, MOBA_BLOCK, MOBA_BLOCK), lambda h: (h, 0, 0, 0)),
        out_shape=jax.ShapeDtypeStruct((A_HEADS, 2, MOBA_BLOCK, MOBA_BLOCK), F32),
        name="rel_bias_tiles",
    )(tab)


def _norm_proj_kernel(h_ref, g_ref, *refs, n_out, col_tile):
    w_refs, o_refs = refs[:n_out], refs[n_out:]
    y = _rms(h_ref[...], g_ref[...]).astype(BF16)
    for w_ref, o_ref in zip(w_refs, o_refs):
        n = w_ref.shape[1]
        for c0 in range(0, n, col_tile):
            c1 = min(c0 + col_tile, n)
            o_ref[:, c0:c1] = _dot(y, w_ref[:, c0:c1]).astype(o_ref.dtype)


def _norm_proj(h2d, g, weights, out_dtypes):
    m, d = h2d.shape
    tm = min(PROJ_ROWS, m)
    n_out = len(weights)
    return pl.pallas_call(
        functools.partial(_norm_proj_kernel, n_out=n_out, col_tile=512),
        grid=(m // tm,),
        in_specs=[pl.BlockSpec((tm, d), lambda i: (i, 0)), _resident((1, d))]
                 + [_resident(w.shape) for w in weights],
        out_specs=[pl.BlockSpec((tm, w.shape[1]), lambda i: (i, 0)) for w in weights],
        out_shape=[jax.ShapeDtypeStruct((m, w.shape[1]), dt) for w, dt in zip(weights, out_dtypes)],
        compiler_params=pltpu.CompilerParams(dimension_semantics=("parallel",), vmem_limit_bytes=VMEM_LIMIT),
        name="norm_in_proj",
    )(h2d, g.reshape(1, d).astype(F32), *weights)


def _moba_kernel(far_ref, q_ref, k_ref, v_ref, bias_ref, o_ref, kmean_ref, vt_ref, addm_ref, s_ref, *, nb):
    blk = MOBA_BLOCK
    hd = A_HEAD_DIM
    hp = pl.program_id(1)
    qi = pl.program_id(2)

    @pl.when(qi == 0)
    def _():
        for j in range(nb):
            kb = k_ref[0, j * blk:(j + 1) * blk, :].astype(F32)
            kmean_ref[j:j + 1, :] = jnp.sum(kb, axis=0, keepdims=True) * (1.0 / blk)
            vt_ref[j] = v_ref[0, j * blk:(j + 1) * blk, :].astype(F32).T.astype(BF16)
        addm_ref[:, nb:, :] = jnp.full((2, HALO, blk), NEG, F32)

    q = q_ref[0]
    lane = lax.broadcasted_iota(I32, (1, 2 * hd), 1)
    rowj = lax.broadcasted_iota(I32, (nb, blk), 0)
    scale = hd ** -0.5
    qms = []
    for hh in range(2):
        in_head = (lane >= hh * hd) & (lane < (hh + 1) * hd)
        qm = jnp.where(in_head, q, jnp.zeros_like(q)) * scale
        qms.append(qm)
        km = jnp.where(in_head, kmean_ref[...], 0.0)
        gate = sum(_dot_nt(part, qm) for part in _split3(km))
        gate = jnp.where(rowj < qi, gate, -jnp.inf)
        ranks = []
        for j in range(nb):
            gj = gate[j:j + 1, :]
            ahead = jnp.where(gate > gj, 1.0, 0.0)
            if j > 0:
                ahead = ahead + jnp.where(rowj < j, jnp.where(gate == gj, 1.0, 0.0), 0.0)
            ranks.append(jnp.sum(ahead, axis=0, keepdims=True))
        rank = jnp.concatenate(ranks, axis=0)
        chosen_bias = jnp.where(rowj == qi - 1, 0.0, far_ref[hp * 2 + hh])
        addm_ref[hh, 0:nb, :] = jnp.where(rowj < qi, jnp.where(rank < float(MOBA_TOPK), chosen_bias, NEG), NEG)

    def k_rows(start, rows):
        return k_ref[0, pl.ds(pl.multiple_of(start, blk), rows), :]

    def softmax_steps(carries, scores, blocks):
        heads = range(2)
        maxes = [functools.reduce(jnp.maximum, [jnp.max(s, axis=0, keepdims=True) for s in scores[hh]])
                 for hh in heads]
        m_new = [jnp.maximum(carries[hh][0], maxes[hh]) for hh in heads]
        ps = [[jnp.exp(s - m_new[hh]) for s in scores[hh]] for hh in heads]
        pv = [sum(_dot(vt_ref[j, hh * hd:(hh + 1) * hd, :], p.astype(BF16)) for j, p in zip(blocks, ps[hh]))
              for hh in heads]
        out = []
        for hh in heads:
            alpha = jnp.exp(carries[hh][0] - m_new[hh])
            l_new = alpha * carries[hh][1] + sum(jnp.sum(p, axis=0, keepdims=True) for p in ps[hh])
            out.append((m_new[hh], l_new, alpha * carries[hh][2] + pv[hh]))
        return out

    pc = jnp.maximum(qi - 1, 0)
    k_prev, k_own = k_rows(pc * blk, blk), k_rows(qi * blk, blk)
    for hh in range(2):
        s_ref[0, hh, 0:blk, :] = _dot_nt(k_prev, qms[hh]) + bias_ref[hh, 1] + addm_ref[hh, pl.ds(pc, 1), :]
        s_ref[0, hh, blk:, :] = _dot_nt(k_own, qms[hh]) + bias_ref[hh, 0]
    n_far = jnp.maximum(qi, 1) // 2

    def item(i, carry):
        t = jnp.minimum(i, nb // 2 - 1)
        j0, j1 = 2 * t, 2 * t + 1
        r1 = jnp.where(j1 <= qi - 2, j1, nb)
        kb = k_rows(j0 * blk, 2 * blk)
        ss = [_dot_nt(kb, qms[hh]) for hh in range(2)]
        slot = lax.rem(i, 2)
        b0 = jnp.where(i == 0, pc, 2 * i - 2)
        b1 = jnp.where(i == 0, qi, 2 * i - 1)
        scores = [[s_ref[slot, hh, 0:blk, :], s_ref[slot, hh, blk:, :]] for hh in range(2)]
        new = softmax_steps([carry[0:3], carry[3:6]], scores, [b0, b1])
        for hh in range(2):
            s_ref[1 - slot, hh, 0:blk, :] = ss[hh][:blk] + addm_ref[hh, pl.ds(j0, 1), :]
            s_ref[1 - slot, hh, blk:, :] = ss[hh][blk:] + addm_ref[hh, pl.ds(r1, 1), :]
        return new[0] + new[1]

    start = (jnp.full((1, blk), -jnp.inf, F32), jnp.zeros((1, blk), F32), jnp.zeros((hd, blk), F32))
    carry = lax.fori_loop(0, n_far + 1, item, start + start)
    outs = [carry[3 * hh + 2] / carry[3 * hh + 1] for hh in range(2)]
    o_ref[0] = jnp.concatenate(outs, axis=0).T.astype(o_ref.dtype)


def _moba(zqkv, bias_tiles, far_bias):
    bsz, s, _ = zqkv.shape
    nb = s // MOBA_BLOCK
    n_pairs = A_HEADS // 2
    pair_w = 2 * A_HEAD_DIM
    return pl.pallas_call(
        functools.partial(_moba_kernel, nb=nb),
        grid=(bsz, n_pairs, nb),
        in_specs=[pl.BlockSpec(memory_space=pltpu.SMEM),
                  pl.BlockSpec((1, MOBA_BLOCK, pair_w), lambda b, hp, qi: (b, qi, hp)),
                  pl.BlockSpec((1, s, pair_w), lambda b, hp, qi: (b, 0, n_pairs + hp)),
                  pl.BlockSpec((1, s, pair_w), lambda b, hp, qi: (b, 0, 2 * n_pairs + hp)),
                  pl.BlockSpec((2, 2, MOBA_BLOCK, MOBA_BLOCK), lambda b, hp, qi: (hp, 0, 0, 0))],
        out_specs=pl.BlockSpec((1, MOBA_BLOCK, pair_w), lambda b, hp, qi: (b, qi, hp)),
        out_shape=jax.ShapeDtypeStruct((bsz, s, A_WIDTH), BF16),
        scratch_shapes=[pltpu.VMEM((nb, pair_w), F32),
                        pltpu.VMEM((nb, pair_w, MOBA_BLOCK), BF16),
                        pltpu.VMEM((2, nb + HALO, MOBA_BLOCK), F32),
                        pltpu.VMEM((2, 2, 2 * MOBA_BLOCK, MOBA_BLOCK), F32)],
        compiler_params=pltpu.CompilerParams(dimension_semantics=("parallel", "parallel", "arbitrary"),
                                             vmem_limit_bytes=VMEM_LIMIT),
        name="moba_attention",
    )(far_bias, zqkv, zqkv, zqkv, bias_tiles)


def _causal_conv(x_ext, w_ref, taps):
    y = w_ref[taps - 1:taps, :] * x_ext
    for back in range(1, taps):
        y = y + w_ref[taps - 1 - back:taps - back, :] * pltpu.roll(x_ext, back, 0)
    return y[HALO:, :]


def _mlstm_kernel(zb_ref, zg_ref, cw_ref, cb_ref, gb_ref, ng_ref, o_ref, hist_ref, state_ref, m_ref):
    L = CHUNK
    dh = B_HEAD_DIM

    @pl.when(pl.program_id(1) == 0)
    def _():
        hist_ref[...] = jnp.zeros_like(hist_ref)
        state_ref[...] = jnp.zeros_like(state_ref)
        m_ref[...] = jnp.zeros_like(m_ref)

    x = zb_ref[0, :, 0:2 * B_WIDTH].astype(F32)
    x_ext = jnp.concatenate([hist_ref[...], x], axis=0)
    hist_ref[...] = x[L - HALO:, :]
    y = _causal_conv(x_ext, cw_ref, MLSTM_CONV) + cb_ref[...]
    y = y * jax.nn.sigmoid(y)
    q_all = y[:, :B_WIDTH]
    k_all = y[:, B_WIDTH:] * (dh ** -0.5)

    g = zg_ref[0] + gb_ref[...]
    lane = lax.broadcasted_iota(I32, (1, 128), 1)
    log_f = jnp.minimum(g, 0.0) - jnp.log(1.0 + jnp.exp(-jnp.abs(g)))
    gates = jnp.where(lane < B_HEADS, g, log_f)
    row = lax.broadcasted_iota(I32, (L, L), 0)
    col = lax.broadcasted_iota(I32, (L, L), 1)
    causal = row >= col
    tril = jnp.where(causal, 1.0, 0.0).astype(BF16)
    triu = jnp.where(row <= col, 1.0, 0.0).astype(BF16)
    csum_col = sum(_dot(tril, part) for part in _split3(gates))
    gates_t = gates.T
    csum_row = sum(_dot(part, triu) for part in _split3(gates_t[0:16, :]))
    unit = jnp.where(lax.broadcasted_iota(I32, (L, dh), 1) == 0, 1.0, 0.0).astype(BF16)

    for h in range(B_HEADS):
        sl = slice(h * dh, (h + 1) * dh)
        b_col = csum_col[:, B_HEADS + h:B_HEADS + h + 1]
        b_row = csum_row[B_HEADS + h:B_HEADS + h + 1, :]
        i_row = gates_t[h:h + 1, :]
        m_prev = m_ref[h]
        log_inter = b_col + m_prev
        log_intra = jnp.where(causal, b_col - b_row + i_row, -jnp.inf)
        m_t = jnp.maximum(log_inter, jnp.max(log_intra, axis=1, keepdims=True))
        w_inter = jnp.exp(log_inter - m_t)
        qh = q_all[:, sl].astype(BF16)
        kh = k_all[:, sl]
        qk = _dot_nt(qh, kh.astype(BF16)) * jnp.exp(log_intra - m_t)
        v_aug = jnp.concatenate([zb_ref[0, :, 2 * B_WIDTH + h * dh:2 * B_WIDTH + (h + 1) * dh], unit], axis=1)
        state = state_ref[h]
        nd = w_inter * _dot(qh, state.astype(BF16)) + _dot(qk.astype(BF16), v_aug)
        den = nd[:, dh:dh + 1]
        hout = nd[:, :dh] / jnp.maximum(jnp.abs(den), jnp.exp(-m_t))

        b_last = b_row[:, L - 1:L]
        log_s = b_last - b_row + i_row
        m_new = jnp.maximum(b_last + m_prev, jnp.max(log_s, axis=1, keepdims=True))
        w_prev = jnp.exp(b_last + m_prev - m_new)
        w_s = jnp.exp(log_s - m_new)
        state_ref[h] = w_prev * state + _dot((kh.T * w_s).astype(BF16), v_aug)
        m_ref[h] = m_new

        hn = _rms(hout, ng_ref[:, sl])
        og = jax.nn.sigmoid(zb_ref[0, :, 3 * B_WIDTH + h * dh:3 * B_WIDTH + (h + 1) * dh].astype(F32))
        o_ref[0, :, sl] = (hn * og).astype(o_ref.dtype)


def _mlstm(zb, zg, conv_w, conv_b, b_i, b_f, norm_g):
    bsz, s, _ = zb.shape
    gate_bias = jnp.zeros((1, 128), F32).at[0, :B_HEADS].set(b_i.astype(F32)).at[0, B_HEADS:2 * B_HEADS].set(
        b_f.astype(F32))
    return pl.pallas_call(
        _mlstm_kernel,
        grid=(bsz, s // CHUNK),
        in_specs=[pl.BlockSpec((1, CHUNK, 4 * B_WIDTH), lambda b, c: (b, c, 0)),
                  pl.BlockSpec((1, CHUNK, 128), lambda b, c: (b, c, 0)),
                  _resident((MLSTM_CONV, 2 * B_WIDTH)), _resident((1, 2 * B_WIDTH)),
                  _resident((1, 128)), _resident((1, B_WIDTH))],
        out_specs=pl.BlockSpec((1, CHUNK, B_WIDTH), lambda b, c: (b, c, 0)),
        out_shape=jax.ShapeDtypeStruct((bsz, s, B_WIDTH), BF16),
        scratch_shapes=[pltpu.VMEM((HALO, 2 * B_WIDTH), F32),
                        pltpu.VMEM((B_HEADS, B_HEAD_DIM, 2 * B_HEAD_DIM), F32),
                        pltpu.VMEM((B_HEADS, 1, 1), F32)],
        compiler_params=pltpu.CompilerParams(dimension_semantics=("parallel", "arbitrary"),
                                             vmem_limit_bytes=VMEM_LIMIT),
        name="mlstm",
    )(zb, zg.reshape(bsz, s, 128), conv_w.astype(F32), conv_b.reshape(1, -1).astype(F32), gate_bias,
      norm_g.reshape(1, -1).astype(F32))


def _pool_ret_kernel(z_ref, cos_ref, sin_ref, swap_ref, dq_ref, dk_ref, dc_ref, dm_ref, bd_ref, pw_ref, ps_ref,
                     rg_ref, o_ref, hist_ref, state_ref):
    L = CHUNK
    c = pl.program_id(1)

    @pl.when(c == 0)
    def _():
        hist_ref[...] = jnp.zeros_like(hist_ref)
        state_ref[...] = jnp.zeros_like(state_ref)

    u = z_ref[0, :, 0:POOL_WIDTH].astype(F32)
    u_ext = jnp.concatenate([hist_ref[...], u], axis=0)
    hist_ref[...] = u[L - POOL_HALO:, :]
    pos1 = (c * L + lax.broadcasted_iota(I32, (L, 1), 0) + 1).astype(F32)
    gd = POOL_GROUP_DIM
    for g, w in enumerate(POOL_WINDOWS):
        sl = slice(g * gd, (g + 1) * gd)
        win = u_ext[:, sl]
        span = 1
        while span < w:
            win = win + pltpu.roll(win, span, 0)
            span *= 2
        pooled = win[POOL_HALO:, :] / jnp.minimum(pos1, float(w)) - u[:, sl]
        o_ref[0, :, sl] = (_dot(pooled.astype(BF16), pw_ref[g]) * ps_ref[:, sl]).astype(o_ref.dtype)

    rq = z_ref[0, :, POOL_WIDTH:POOL_WIDTH + R_QK_WIDTH]
    rk = z_ref[0, :, POOL_WIDTH + R_QK_WIDTH:POOL_WIDTH + 2 * R_QK_WIDTH]
    v0 = POOL_WIDTH + 2 * R_QK_WIDTH
    rv = z_ref[0, :, v0:v0 + R_V_WIDTH]
    cosv, sinv = cos_ref[...], sin_ref[...]
    q_rot = rq.astype(F32) * cosv + _dot(rq, swap_ref[...]) * sinv
    k_rot = (rk.astype(F32) * cosv + _dot(rk, swap_ref[...]) * sinv) * (R_QK_DIM ** -0.5)
    qb = q_rot.astype(BF16)
    kb = k_rot.astype(BF16)
    inter = _dot((q_rot * dq_ref[...]).astype(BF16), state_ref[...].astype(BF16))
    lane = lax.broadcasted_iota(I32, (1, R_QK_WIDTH), 1)
    g0 = v0 + R_V_WIDTH
    for h in range(R_HEADS):
        in_head = (lane >= h * R_QK_DIM) & (lane < (h + 1) * R_QK_DIM)
        sc = _dot_nt(jnp.where(in_head, qb, jnp.zeros_like(qb)), kb) * dm_ref[h]
        vs = slice(h * R_V_DIM, (h + 1) * R_V_DIM)
        y = _dot(sc.astype(BF16), rv[:, vs]) + inter[:, vs]
        gate = z_ref[0, :, g0 + h * R_V_DIM:g0 + (h + 1) * R_V_DIM].astype(F32)
        gate = gate * jax.nn.sigmoid(gate)
        o_ref[0, :, POOL_WIDTH + h * R_V_DIM:POOL_WIDTH + (h + 1) * R_V_DIM] = (
            _rms(y, rg_ref[:, vs]) * gate).astype(o_ref.dtype)
    upd = _dot((k_rot * dk_ref[...]).T.astype(BF16), rv)
    state_ref[...] = dc_ref[...] * state_ref[...] + upd * bd_ref[...]


def _retention_tables(s):
    L = CHUNK
    half = R_QK_DIM // 2
    inv = ROPE_BASE ** (-jnp.arange(half, dtype=F32) / half)
    ang = jnp.arange(s, dtype=F32)[:, None] * inv[None, :]
    cos, sin = jnp.cos(ang), jnp.sin(ang)
    cos_t = jnp.tile(jnp.concatenate([cos, cos], axis=-1), (1, R_HEADS))
    sin_t = jnp.tile(jnp.concatenate([-sin, sin], axis=-1), (1, R_HEADS))
    j = np.arange(R_QK_WIDTH)
    src = (j // R_QK_DIM) * R_QK_DIM + (j % R_QK_DIM + half) % R_QK_DIM
    swap = np.zeros((R_QK_WIDTH, R_QK_WIDTH), np.float32)
    swap[src, j] = 1.0
    log_g = jnp.log(1.0 - jnp.exp2(-5.0 - jnp.arange(R_HEADS, dtype=F32)))
    t = jnp.arange(L, dtype=F32)
    rel = t[:, None] - t[None, :]
    decay_intra = jnp.where(rel >= 0, jnp.exp(jnp.maximum(rel, 0.0) * log_g[:, None, None]), 0.0)
    decay_q = jnp.exp((t + 1.0) * log_g[:, None])
    decay_k = jnp.exp((L - 1.0 - t) * log_g[:, None])
    decay_chunk = jnp.exp(L * log_g)
    dq = jnp.repeat(decay_q.T, R_QK_DIM, axis=1)
    dk = jnp.repeat(decay_k.T, R_QK_DIM, axis=1)
    dc = jnp.repeat(decay_chunk, R_QK_DIM)[:, None]
    bd = (np.arange(R_QK_WIDTH)[:, None] // R_QK_DIM == np.arange(R_V_WIDTH)[None, :] // R_V_DIM)
    return cos_t, sin_t, jnp.asarray(swap, BF16), dq, dk, dc, decay_intra, jnp.asarray(bd, F32)


def _pool_ret(z, pool_w, pool_scale, ret_g):
    bsz, s, width = z.shape
    cos_t, sin_t, swap, dq, dk, dc, dm, bd = _retention_tables(s)
    return pl.pallas_call(
        _pool_ret_kernel,
        grid=(bsz, s // CHUNK),
        in_specs=[pl.BlockSpec((1, CHUNK, width), lambda b, c: (b, c, 0)),
                  pl.BlockSpec((CHUNK, R_QK_WIDTH), lambda b, c: (c, 0)),
                  pl.BlockSpec((CHUNK, R_QK_WIDTH), lambda b, c: (c, 0)),
                  _resident(swap.shape), _resident(dq.shape), _resident(dk.shape), _resident(dc.shape),
                  _resident(dm.shape), _resident(bd.shape), _resident(pool_w.shape),
                  _resident((1, POOL_WIDTH)), _resident((1, R_V_WIDTH))],
        out_specs=pl.BlockSpec((1, CHUNK, POOL_WIDTH + R_V_WIDTH), lambda b, c: (b, c, 0)),
        out_shape=jax.ShapeDtypeStruct((bsz, s, POOL_WIDTH + R_V_WIDTH), BF16),
        scratch_shapes=[pltpu.VMEM((POOL_HALO, POOL_WIDTH), F32),
                        pltpu.VMEM((R_QK_WIDTH, R_V_WIDTH), F32)],
        compiler_params=pltpu.CompilerParams(dimension_semantics=("parallel", "arbitrary"),
                                             vmem_limit_bytes=VMEM_LIMIT),
        name="pool_retention",
    )(z, cos_t, sin_t, swap, dq, dk, dc, dm, bd, pool_w.astype(BF16), pool_scale.reshape(1, -1).astype(F32),
      ret_g.reshape(1, -1).astype(F32))


def _ffn_kernel(h_ref, a1_ref, a2_ref, wo1_ref, wo2_ref, g_ref, wup_ref, cw_ref, cb_ref, wd_ref, gout_ref, o_ref,
                hist_ref, acc_ref, y_ref, x_ref, *, n_tiles, final_norm):
    tm = h_ref.shape[1]
    lanes = x_ref.shape[2]
    n_lane_tiles = FF_TILE // lanes

    @pl.when(pl.program_id(1) == 0)
    def _():
        hist_ref[...] = jnp.zeros_like(hist_ref)

    h1 = h_ref[0] + _dot(a1_ref[0], wo1_ref[...]) + _dot(a2_ref[0], wo2_ref[...])
    y_ref[...] = _rms(h1, g_ref[...]).astype(BF16)
    acc_ref[...] = h1

    def conv(u, idx, slot):
        w = cw_ref[idx]
        outs = []
        for k in range(n_lane_tiles):
            ls = slice(k * lanes, (k + 1) * lanes)
            xr = x_ref.at[slot * n_lane_tiles + k]
            xr[0:HALO, :] = hist_ref[idx, :, ls]
            xr[HALO:, :] = u[:, ls]
            hist_ref[idx, :, ls] = u[tm - HALO:, ls]
            outs.append(w[2:3, ls] * u[:, ls] + w[1:2, ls] * xr[HALO - 1:HALO - 1 + tm, :]
                        + w[0:1, ls] * xr[HALO - 2:HALO - 2 + tm, :])
        return jnp.concatenate(outs, axis=1) + cb_ref[idx]

    def tile(c, _):
        y = y_ref[...]
        slot = 2 * lax.rem(c, 2)
        gate = conv(_dot(y, wup_ref[c]), c, slot)
        val = conv(_dot(y, wup_ref[n_tiles + c]), n_tiles + c, slot + 1)
        act = (gate * jax.nn.sigmoid(gate) * val).astype(BF16)
        acc_ref[...] += _dot(act, wd_ref[c])
        return 0

    lax.fori_loop(0, n_tiles, tile, 0, unroll=2)
    out = acc_ref[...]
    if final_norm:
        out = _rms(out, gout_ref[...])
    o_ref[0] = out


def _ffn(h, a1, a2, col1, col2, w_out, norm_g, w_up, conv_w, conv_b, w_down, out_g, final_norm):
    bsz, s, d = h.shape
    tm = min(FFN_ROWS, s)
    half = w_out.shape[0] // 2
    n_tiles = D_FF // FF_TILE
    wo = w_out.astype(BF16)
    wup = w_up.astype(BF16).reshape(d, 2 * n_tiles, FF_TILE).transpose(1, 0, 2)
    cw = jnp.zeros((2 * n_tiles, HALO, FF_TILE), F32).at[:, :FFN_CONV, :].set(
        conv_w.astype(F32).reshape(FFN_CONV, 2 * n_tiles, FF_TILE).transpose(1, 0, 2))
    cb = conv_b.astype(F32).reshape(2 * n_tiles, 1, FF_TILE)
    wd = w_down.astype(BF16).reshape(n_tiles, FF_TILE, d)
    return pl.pallas_call(
        functools.partial(_ffn_kernel, n_tiles=n_tiles, final_norm=final_norm),
        grid=(bsz, s // tm),
        in_specs=[pl.BlockSpec((1, tm, d), lambda b, t: (b, t, 0)),
                  pl.BlockSpec((1, tm, half), lambda b, t: (b, t, col1)),
                  pl.BlockSpec((1, tm, half), lambda b, t: (b, t, col2)),
                  _resident((half, d)), _resident((half, d)), _resident((1, d)),
                  _resident(wup.shape), _resident(cw.shape), _resident(cb.shape), _resident(wd.shape),
                  _resident((1, d))],
        out_specs=pl.BlockSpec((1, tm, d), lambda b, t: (b, t, 0)),
        out_shape=jax.ShapeDtypeStruct((bsz, s, d), F32),
        scratch_shapes=[pltpu.VMEM((2 * n_tiles, HALO, FF_TILE), F32),
                        pltpu.VMEM((tm, d), F32),
                        pltpu.VMEM((tm, d), BF16),
                        pltpu.VMEM((4 * FF_TILE // 128, HALO + tm, 128), F32)],
        compiler_params=pltpu.CompilerParams(dimension_semantics=("parallel", "arbitrary"),
                                             vmem_limit_bytes=VMEM_LIMIT),
        name="out_proj_ffn",
    )(h, a1, a2, wo[:half], wo[half:], norm_g.reshape(1, d).astype(F32), wup, cw, cb, wd,
      out_g.reshape(1, d).astype(F32))


def kernel(x, rel_bias, norm_mix_g, norm_ffn_g, norm_out_g, w_in_ab, w_out_ab, mlstm_conv_w, mlstm_conv_b, mlstm_b_i,
           mlstm_b_f, mlstm_norm_g, w_in_cd, w_out_cd, pool_w, pool_scale, ret_norm_g, ffn_w_up, ffn_conv_w,
           ffn_conv_b, ffn_w_down):
    bsz, s, d = x.shape
    m = bsz * s

    w0 = w_in_ab[0].astype(BF16)
    n_attn = 3 * A_WIDTH
    n_main = n_attn + 4 * B_WIDTH
    w_gate = jnp.zeros((d, 128), BF16).at[:, :2 * B_HEADS].set(w0[:, n_main:])
    zqkv, zb, zg = _norm_proj(x.reshape(m, d), norm_mix_g[0], [w0[:, :n_attn], w0[:, n_attn:n_main], w_gate],
                              [BF16, BF16, F32])
    bias_tiles = _bias_tables(rel_bias)
    far_bias = rel_bias[REL_BUCKETS - 1].astype(F32)
    ya = _moba(zqkv.reshape(bsz, s, n_attn), bias_tiles, far_bias)
    yb = _mlstm(zb.reshape(bsz, s, 4 * B_WIDTH), zg, mlstm_conv_w[0], mlstm_conv_b[0], mlstm_b_i[0], mlstm_b_f[0],
                mlstm_norm_g[0])
    h = _ffn(x, ya, yb, 0, 0, w_out_ab[0], norm_ffn_g[0], ffn_w_up[0], ffn_conv_w[0], ffn_conv_b[0], ffn_w_down[0],
             norm_out_g, False)

    (z1,) = _norm_proj(h.reshape(m, d), norm_mix_g[1], [w_in_cd[0].astype(BF16)], [BF16])
    cat = _pool_ret(z1.reshape(bsz, s, -1), pool_w[0], pool_scale[0], ret_norm_g[0])
    return _ffn(h, cat, cat, 0, 1, w_out_cd[0], norm_ffn_g[1], ffn_w_up[1], ffn_conv_w[1], ffn_conv_b[1],
                ffn_w_down[1], norm_out_g, True)
```

```python
import functools
import math

import numpy as np
import jax
import jax.numpy as jnp
from jax import lax
from jax.experimental import pallas as pl
from jax.experimental.pallas import tpu as pltpu

F32 = jnp.float32
BF16 = jnp.bfloat16
I32 = jnp.int32

EPS = 1e-6
NEG = -1e30

D_MODEL = 1024
A_HEADS = 8
A_HEAD_DIM = 64
A_WIDTH = 512
MOBA_BLOCK = 256
MOBA_TOPK = 3
REL_BUCKETS = 32
REL_MAX_DIST = 128
B_HEADS = 4
B_HEAD_DIM = 128
B_WIDTH = 512
MLSTM_CONV = 4
POOL_WINDOWS = (2, 4, 8, 16)
POOL_GROUP_DIM = 128
POOL_WIDTH = 512
R_HEADS = 4
R_QK_DIM = 64
R_V_DIM = 128
R_QK_WIDTH = 256
R_V_WIDTH = 512
ROPE_BASE = 10000.0
D_FF = 2816
FFN_CONV = 3

CHUNK = 256
ONES_ROWS = 16
HALO = 8
POOL_HALO = 16
FF_TILE = 256
PROJ_ROWS = 512
FFN_ROWS = 1024
VMEM_LIMIT = 56 * 1024 * 1024

_NT = (((1,), (1,)), ((), ()))


def _dot(a, b):
    return jnp.dot(a, b, preferred_element_type=F32)


def _dot_nt(a, b):
    return lax.dot_general(a, b, _NT, preferred_element_type=F32)


def _split3(a):
    a1 = a.astype(BF16)
    r = a - a1.astype(F32)
    a2 = r.astype(BF16)
    a3 = (r - a2.astype(F32)).astype(BF16)
    return a1, a2, a3


def _rms(x, g):
    return x * lax.rsqrt(jnp.mean(x * x, axis=-1, keepdims=True) + EPS) * g


def _resident(shape):
    n = len(shape)
    return pl.BlockSpec(shape, lambda *_: (0,) * n, pipeline_mode=pl.Buffered(1))


def _bucket_thresholds():
    max_exact = REL_BUCKETS // 2
    d = np.arange(1, 4 * REL_MAX_DIST, dtype=np.float32)
    large = max_exact + (np.log(d / np.float32(max_exact)) / np.float32(math.log(REL_MAX_DIST / max_exact))
                         * np.float32(REL_BUCKETS - max_exact)).astype(np.int32)
    large = np.minimum(large, REL_BUCKETS - 1)
    return [int(d[np.argmax(large >= b)]) for b in range(max_exact + 1, REL_BUCKETS)]


def _bias_kernel(tab_ref, o_ref, *, thresholds):
    h = pl.program_id(0)
    row = lax.broadcasted_iota(I32, (MOBA_BLOCK, MOBA_BLOCK), 0)
    col = lax.broadcasted_iota(I32, (MOBA_BLOCK, MOBA_BLOCK), 1)
    max_exact = REL_BUCKETS // 2
    for t, off in enumerate((0, MOBA_BLOCK)):
        dist = col - row + off
        d = jnp.maximum(dist, 0)
        large = jnp.full(d.shape, max_exact, I32)
        for thr in thresholds:
            large = large + jnp.where(d >= thr, 1, 0)
        bucket = jnp.where(d < max_exact, d, large)
        val = jnp.zeros(d.shape, F32)
        for i in range(REL_BUCKETS):
            val = jnp.where(bucket == i, tab_ref[h, i], val)
        if off == 0:
            val = jnp.where(dist >= 0, val, NEG)
        o_ref[0, t] = val


def _bias_tables(rel_bias):
    tab = rel_bias.astype(F32).T
    return pl.pallas_call(
        functools.partial(_bias_kernel, thresholds=_bucket_thresholds()),
        grid=(A_HEADS,),
        in_specs=[pl.BlockSpec(memory_space=pltpu.SMEM)],
        out_specs=pl.BlockSpec((1, 2, MOBA_BLOCK, MOBA_BLOCK), lambda h: (h, 0, 0, 0)),
        out_shape=jax.ShapeDtypeStruct((A_HEADS, 2, MOBA_BLOCK, MOBA_BLOCK), F32),
        name="rel_bias_tiles",
    )(tab)


def _norm_proj_kernel(h_ref, g_ref, w_ref, *o_refs, col_tile):
    y = _rms(h_ref[...], g_ref[...]).astype(BF16)
    base = 0
    for o_ref in o_refs:
        n = o_ref.shape[1]
        for c0 in range(0, n, col_tile):
            c1 = min(c0 + col_tile, n)
            o_ref[:, c0:c1] = _dot(y, w_ref[:, base + c0:base + c1]).astype(o_ref.dtype)
        base += n


def _norm_proj(h2d, g, w, out_widths, out_dtypes):
    m, d = h2d.shape
    tm = min(PROJ_ROWS, m)
    assert sum(out_widths) == w.shape[1]
    return pl.pallas_call(
        functools.partial(_norm_proj_kernel, col_tile=512),
        grid=(m // tm,),
        in_specs=[pl.BlockSpec((tm, d), lambda i: (i, 0)), _resident((1, d)), _resident(w.shape)],
        out_specs=[pl.BlockSpec((tm, n), lambda i: (i, 0)) for n in out_widths],
        out_shape=[jax.ShapeDtypeStruct((m, n), dt) for n, dt in zip(out_widths, out_dtypes)],
        compiler_params=pltpu.CompilerParams(dimension_semantics=("parallel",), vmem_limit_bytes=VMEM_LIMIT),
        name="norm_in_proj",
    )(h2d, g.reshape(1, d).astype(F32), w)


def _moba_kernel(far_ref, q_ref, k_ref, v_ref, bias_ref, o_ref, kmean_ref, vt_ref, addm_ref, s_ref, *, nb):
    blk = MOBA_BLOCK
    hd = A_HEAD_DIM
    hp = pl.program_id(1)
    qi = pl.program_id(2)

    @pl.when(qi == 0)
    def _():
        for j in range(nb):
            kb = k_ref[0, j * blk:(j + 1) * blk, :].astype(F32)
            kmean_ref[j:j + 1, :] = jnp.sum(kb, axis=0, keepdims=True) * (1.0 / blk)
            vt = v_ref[0, j * blk:(j + 1) * blk, :].astype(F32).T.astype(BF16)
            for hh in range(2):
                vt_ref[j, hh, 0:hd, :] = vt[hh * hd:(hh + 1) * hd, :]
                vt_ref[j, hh, hd:, :] = jnp.ones((ONES_ROWS, blk), BF16)
        addm_ref[:, nb:, :] = jnp.full((2, HALO, blk), NEG, F32)

    q = q_ref[0]
    lane = lax.broadcasted_iota(I32, (1, 2 * hd), 1)
    rowj = lax.broadcasted_iota(I32, (nb, blk), 0)
    scale = hd ** -0.5
    qms = []
    for hh in range(2):
        in_head = (lane >= hh * hd) & (lane < (hh + 1) * hd)
        qm = jnp.where(in_head, q, jnp.zeros_like(q)) * scale
        qms.append(qm)
        km = jnp.where(in_head, kmean_ref[...], 0.0)
        gate = sum(_dot_nt(part, qm) for part in _split3(km))
        gate = jnp.where(rowj < qi, gate, -jnp.inf)
        ranks = []
        for j in range(nb):
            gj = gate[j:j + 1, :]
            ahead = jnp.where(gate > gj, 1.0, 0.0)
            if j > 0:
                ahead = ahead + jnp.where(rowj < j, jnp.where(gate == gj, 1.0, 0.0), 0.0)
            ranks.append(jnp.sum(ahead, axis=0, keepdims=True))
        rank = jnp.concatenate(ranks, axis=0)
        chosen_bias = jnp.where(rowj == qi - 1, 0.0, far_ref[hp * 2 + hh])
        addm_ref[hh, 0:nb, :] = jnp.where(rowj < qi, jnp.where(rank < float(MOBA_TOPK), chosen_bias, NEG), NEG)

    def k_rows(start, rows):
        return k_ref[0, pl.ds(pl.multiple_of(start, blk), rows), :]

    def item(i, slot, carry):
        t = jnp.minimum(i, nb // 2 - 1)
        j0, j1 = 2 * t, 2 * t + 1
        r1 = jnp.where(j1 <= qi - 2, j1, nb)
        kb = k_rows(j0 * blk, 2 * blk)
        ss = [_dot_nt(kb, qms[hh]) for hh in range(2)]
        blocks = (jnp.where(i == 0, pc, 2 * i - 2), jnp.where(i == 0, qi, 2 * i - 1))
        scores = [s_ref[slot, hh] for hh in range(2)]
        m_new = [jnp.maximum(carry[2 * hh], jnp.max(scores[hh], axis=0, keepdims=True)) for hh in range(2)]
        ps = [jnp.exp(scores[hh] - m_new[hh]).astype(BF16) for hh in range(2)]
        pv = [sum(_dot(vt_ref[j, hh], ps[hh][n * blk:(n + 1) * blk, :]) for n, j in enumerate(blocks))
              for hh in range(2)]
        out = ()
        for hh in range(2):
            alpha = jnp.exp(carry[2 * hh] - m_new[hh])
            out += (m_new[hh], alpha * carry[2 * hh + 1] + pv[hh])
        for hh in range(2):
            s_ref[1 - slot, hh, 0:blk, :] = ss[hh][:blk] + addm_ref[hh, pl.ds(j0, 1), :]
            s_ref[1 - slot, hh, blk:, :] = ss[hh][blk:] + addm_ref[hh, pl.ds(r1, 1), :]
        return out

    pc = jnp.maximum(qi - 1, 0)
    k_prev, k_own = k_rows(pc * blk, blk), k_rows(qi * blk, blk)
    for hh in range(2):
        s_ref[0, hh, 0:blk, :] = _dot_nt(k_prev, qms[hh]) + bias_ref[hh, 1] + addm_ref[hh, pl.ds(pc, 1), :]
        s_ref[0, hh, blk:, :] = _dot_nt(k_own, qms[hh]) + bias_ref[hh, 0]
    n_items = jnp.maximum(qi, 1) // 2 + 1

    start = (jnp.full((1, blk), -jnp.inf, F32), jnp.zeros((hd + ONES_ROWS, blk), F32))
    carry = lax.fori_loop(0, n_items // 2, lambda t, c: item(2 * t + 1, 1, item(2 * t, 0, c)), start + start)
    carry = lax.cond(lax.rem(n_items, 2) == 1, lambda c: item(n_items - 1, 0, c), lambda c: c, carry)
    outs = [carry[2 * hh + 1][:hd] / carry[2 * hh + 1][hd:hd + 1] for hh in range(2)]
    o_ref[0] = jnp.concatenate(outs, axis=0).T.astype(o_ref.dtype)


def _moba(zqkv, bias_tiles, far_bias):
    bsz, s, _ = zqkv.shape
    nb = s // MOBA_BLOCK
    n_pairs = A_HEADS // 2
    pair_w = 2 * A_HEAD_DIM
    return pl.pallas_call(
        functools.partial(_moba_kernel, nb=nb),
        grid=(bsz, n_pairs, nb),
        in_specs=[pl.BlockSpec(memory_space=pltpu.SMEM),
                  pl.BlockSpec((1, MOBA_BLOCK, pair_w), lambda b, hp, qi: (b, qi, hp)),
                  pl.BlockSpec((1, s, pair_w), lambda b, hp, qi: (b, 0, n_pairs + hp)),
                  pl.BlockSpec((1, s, pair_w), lambda b, hp, qi: (b, 0, 2 * n_pairs + hp)),
                  pl.BlockSpec((2, 2, MOBA_BLOCK, MOBA_BLOCK), lambda b, hp, qi: (hp, 0, 0, 0))],
        out_specs=pl.BlockSpec((1, MOBA_BLOCK, pair_w), lambda b, hp, qi: (b, qi, hp)),
        out_shape=jax.ShapeDtypeStruct((bsz, s, A_WIDTH), BF16),
        scratch_shapes=[pltpu.VMEM((nb, pair_w), F32),
                        pltpu.VMEM((nb, 2, A_HEAD_DIM + ONES_ROWS, MOBA_BLOCK), BF16),
                        pltpu.VMEM((2, nb + HALO, MOBA_BLOCK), F32),
                        pltpu.VMEM((2, 2, 2 * MOBA_BLOCK, MOBA_BLOCK), F32)],
        compiler_params=pltpu.CompilerParams(dimension_semantics=("parallel", "parallel", "arbitrary"),
                                             vmem_limit_bytes=VMEM_LIMIT),
        name="moba_attention",
    )(far_bias, zqkv, zqkv, zqkv, bias_tiles)


def _causal_conv(x_ext, w_ref, taps):
    y = w_ref[taps - 1:taps, :] * x_ext
    for back in range(1, taps):
        y = y + w_ref[taps - 1 - back:taps - back, :] * pltpu.roll(x_ext, back, 0)
    return y[HALO:, :]


def _mlstm_kernel(zb_ref, zg_ref, cw_ref, cb_ref, gb_ref, ng_ref, o_ref, hist_ref, state_ref, m_ref):
    L = CHUNK
    dh = B_HEAD_DIM

    @pl.when(pl.program_id(1) == 0)
    def _():
        hist_ref[...] = jnp.zeros_like(hist_ref)
        state_ref[...] = jnp.zeros_like(state_ref)
        m_ref[...] = jnp.zeros_like(m_ref)

    x = zb_ref[0, :, 0:2 * B_WIDTH].astype(F32)
    x_ext = jnp.concatenate([hist_ref[...], x], axis=0)
    hist_ref[...] = x[L - HALO:, :]
    y = _causal_conv(x_ext, cw_ref, MLSTM_CONV) + cb_ref[...]
    y = y * jax.nn.sigmoid(y)
    q_all = y[:, :B_WIDTH]
    k_all = y[:, B_WIDTH:] * (dh ** -0.5)

    g = zg_ref[0] + gb_ref[...]
    lane = lax.broadcasted_iota(I32, (1, 128), 1)
    log_f = jnp.minimum(g, 0.0) - jnp.log(1.0 + jnp.exp(-jnp.abs(g)))
    gates = jnp.where(lane < B_HEADS, g, log_f)
    row = lax.broadcasted_iota(I32, (L, L), 0)
    col = lax.broadcasted_iota(I32, (L, L), 1)
    causal = row >= col
    tril = jnp.where(causal, 1.0, 0.0).astype(BF16)
    triu = jnp.where(row <= col, 1.0, 0.0).astype(BF16)
    csum_col = sum(_dot(tril, part) for part in _split3(gates))
    gates_t = gates.T
    csum_row = sum(_dot(part, triu) for part in _split3(gates_t[0:16, :]))
    unit = jnp.where(lax.broadcasted_iota(I32, (L, dh), 1) == 0, 1.0, 0.0).astype(BF16)

    for h in range(B_HEADS):
        sl = slice(h * dh, (h + 1) * dh)
        b_col = csum_col[:, B_HEADS + h:B_HEADS + h + 1]
        b_row = csum_row[B_HEADS + h:B_HEADS + h + 1, :]
        i_row = gates_t[h:h + 1, :]
        m_prev = m_ref[h]
        log_inter = b_col + m_prev
        log_intra = jnp.where(causal, b_col - b_row + i_row, -jnp.inf)
        m_t = jnp.maximum(log_inter, jnp.max(log_intra, axis=1, keepdims=True))
        w_inter = jnp.exp(log_inter - m_t)
        qh = q_all[:, sl].astype(BF16)
        kh = k_all[:, sl]
        qk = _dot_nt(qh, kh.astype(BF16)) * jnp.exp(log_intra - m_t)
        v_aug = jnp.concatenate([zb_ref[0, :, 2 * B_WIDTH + h * dh:2 * B_WIDTH + (h + 1) * dh], unit], axis=1)
        state = state_ref[h]
        nd = w_inter * _dot(qh, state.astype(BF16)) + _dot(qk.astype(BF16), v_aug)
        den = nd[:, dh:dh + 1]
        hout = nd[:, :dh] / jnp.maximum(jnp.abs(den), jnp.exp(-m_t))

        b_last = b_row[:, L - 1:L]
        log_s = b_last - b_row + i_row
        m_new = jnp.maximum(b_last + m_prev, jnp.max(log_s, axis=1, keepdims=True))
        w_prev = jnp.exp(b_last + m_prev - m_new)
        w_s = jnp.exp(log_s - m_new)
        state_ref[h] = w_prev * state + _dot((kh.T * w_s).astype(BF16), v_aug)
        m_ref[h] = m_new

        hn = _rms(hout, ng_ref[:, sl])
        og = jax.nn.sigmoid(zb_ref[0, :, 3 * B_WIDTH + h * dh:3 * B_WIDTH + (h + 1) * dh].astype(F32))
        o_ref[0, :, sl] = (hn * og).astype(o_ref.dtype)


def _mlstm(zb, zg, conv_w, conv_b, b_i, b_f, norm_g):
    bsz, s, _ = zb.shape
    gate_bias = jnp.zeros((1, 128), F32).at[0, :B_HEADS].set(b_i.astype(F32)).at[0, B_HEADS:2 * B_HEADS].set(
        b_f.astype(F32))
    return pl.pallas_call(
        _mlstm_kernel,
        grid=(bsz, s // CHUNK),
        in_specs=[pl.BlockSpec((1, CHUNK, 4 * B_WIDTH), lambda b, c: (b, c, 0)),
                  pl.BlockSpec((1, CHUNK, 128), lambda b, c: (b, c, 0)),
                  _resident((MLSTM_CONV, 2 * B_WIDTH)), _resident((1, 2 * B_WIDTH)),
                  _resident((1, 128)), _resident((1, B_WIDTH))],
        out_specs=pl.BlockSpec((1, CHUNK, B_WIDTH), lambda b, c: (b, c, 0)),
        out_shape=jax.ShapeDtypeStruct((bsz, s, B_WIDTH), BF16),
        scratch_shapes=[pltpu.VMEM((HALO, 2 * B_WIDTH), F32),
                        pltpu.VMEM((B_HEADS, B_HEAD_DIM, 2 * B_HEAD_DIM), F32),
                        pltpu.VMEM((B_HEADS, 1, 1), F32)],
        compiler_params=pltpu.CompilerParams(dimension_semantics=("parallel", "arbitrary"),
                                             vmem_limit_bytes=VMEM_LIMIT),
        name="mlstm",
    )(zb, zg.reshape(bsz, s, 128), conv_w.astype(F32), conv_b.reshape(1, -1).astype(F32), gate_bias,
      norm_g.reshape(1, -1).astype(F32))


def _pool_ret_kernel(z_ref, cos_ref, sin_ref, swap_ref, dq_ref, dk_ref, dc_ref, dm_ref, bd_ref, pw_ref, ps_ref,
                     rg_ref, o_ref, hist_ref, state_ref):
    L = CHUNK
    c = pl.program_id(1)

    @pl.when(c == 0)
    def _():
        hist_ref[...] = jnp.zeros_like(hist_ref)
        state_ref[...] = jnp.zeros_like(state_ref)

    u = z_ref[0, :, 0:POOL_WIDTH].astype(F32)
    u_ext = jnp.concatenate([hist_ref[...], u], axis=0)
    hist_ref[...] = u[L - POOL_HALO:, :]
    pos1 = (c * L + lax.broadcasted_iota(I32, (L, 1), 0) + 1).astype(F32)
    gd = POOL_GROUP_DIM
    for g, w in enumerate(POOL_WINDOWS):
        sl = slice(g * gd, (g + 1) * gd)
        win = u_ext[:, sl]
        span = 1
        while span < w:
            win = win + pltpu.roll(win, span, 0)
            span *= 2
        pooled = win[POOL_HALO:, :] / jnp.minimum(pos1, float(w)) - u[:, sl]
        o_ref[0, :, sl] = (_dot(pooled.astype(BF16), pw_ref[g]) * ps_ref[:, sl]).astype(o_ref.dtype)

    rq = z_ref[0, :, POOL_WIDTH:POOL_WIDTH + R_QK_WIDTH]
    rk = z_ref[0, :, POOL_WIDTH + R_QK_WIDTH:POOL_WIDTH + 2 * R_QK_WIDTH]
    v0 = POOL_WIDTH + 2 * R_QK_WIDTH
    rv = z_ref[0, :, v0:v0 + R_V_WIDTH]
    cosv, sinv = cos_ref[...], sin_ref[...]
    q_rot = rq.astype(F32) * cosv + _dot(rq, swap_ref[...]) * sinv
    k_rot = (rk.astype(F32) * cosv + _dot(rk, swap_ref[...]) * sinv) * (R_QK_DIM ** -0.5)
    qb = q_rot.astype(BF16)
    kb = k_rot.astype(BF16)
    inter = _dot((q_rot * dq_ref[...]).astype(BF16), state_ref[...].astype(BF16))
    lane = lax.broadcasted_iota(I32, (1, R_QK_WIDTH), 1)
    g0 = v0 + R_V_WIDTH
    for h in range(R_HEADS):
        in_head = (lane >= h * R_QK_DIM) & (lane < (h + 1) * R_QK_DIM)
        sc = _dot_nt(jnp.where(in_head, qb, jnp.zeros_like(qb)), kb) * dm_ref[h]
        vs = slice(h * R_V_DIM, (h + 1) * R_V_DIM)
        y = _dot(sc.astype(BF16), rv[:, vs]) + inter[:, vs]
        gate = z_ref[0, :, g0 + h * R_V_DIM:g0 + (h + 1) * R_V_DIM].astype(F32)
        gate = gate * jax.nn.sigmoid(gate)
        o_ref[0, :, POOL_WIDTH + h * R_V_DIM:POOL_WIDTH + (h + 1) * R_V_DIM] = (
            _rms(y, rg_ref[:, vs]) * gate).astype(o_ref.dtype)
    upd = _dot((k_rot * dk_ref[...]).T.astype(BF16), rv)
    state_ref[...] = dc_ref[...] * state_ref[...] + upd * bd_ref[...]


def _retention_tables(s):
    L = CHUNK
    half = R_QK_DIM // 2
    inv = ROPE_BASE ** (-jnp.arange(half, dtype=F32) / half)
    ang = jnp.arange(s, dtype=F32)[:, None] * inv[None, :]
    cos, sin = jnp.cos(ang), jnp.sin(ang)
    cos_t = jnp.tile(jnp.concatenate([cos, cos], axis=-1), (1, R_HEADS))
    sin_t = jnp.tile(jnp.concatenate([-sin, sin], axis=-1), (1, R_HEADS))
    j = np.arange(R_QK_WIDTH)
    src = (j // R_QK_DIM) * R_QK_DIM + (j % R_QK_DIM + half) % R_QK_DIM
    swap = np.zeros((R_QK_WIDTH, R_QK_WIDTH), np.float32)
    swap[src, j] = 1.0
    log_g = jnp.log(1.0 - jnp.exp2(-5.0 - jnp.arange(R_HEADS, dtype=F32)))
    t = jnp.arange(L, dtype=F32)
    rel = t[:, None] - t[None, :]
    decay_intra = jnp.where(rel >= 0, jnp.exp(jnp.maximum(rel, 0.0) * log_g[:, None, None]), 0.0)
    decay_q = jnp.exp((t + 1.0) * log_g[:, None])
    decay_k = jnp.exp((L - 1.0 - t) * log_g[:, None])
    decay_chunk = jnp.exp(L * log_g)
    dq = jnp.repeat(decay_q.T, R_QK_DIM, axis=1)
    dk = jnp.repeat(decay_k.T, R_QK_DIM, axis=1)
    dc = jnp.repeat(decay_chunk, R_QK_DIM)[:, None]
    bd = (np.arange(R_QK_WIDTH)[:, None] // R_QK_DIM == np.arange(R_V_WIDTH)[None, :] // R_V_DIM)
    return cos_t, sin_t, jnp.asarray(swap, BF16), dq, dk, dc, decay_intra, jnp.asarray(bd, F32)


def _pool_ret(z, pool_w, pool_scale, ret_g):
    bsz, s, width = z.shape
    cos_t, sin_t, swap, dq, dk, dc, dm, bd = _retention_tables(s)
    return pl.pallas_call(
        _pool_ret_kernel,
        grid=(bsz, s // CHUNK),
        in_specs=[pl.BlockSpec((1, CHUNK, width), lambda b, c: (b, c, 0)),
                  pl.BlockSpec((CHUNK, R_QK_WIDTH), lambda b, c: (c, 0)),
                  pl.BlockSpec((CHUNK, R_QK_WIDTH), lambda b, c: (c, 0)),
                  _resident(swap.shape), _resident(dq.shape), _resident(dk.shape), _resident(dc.shape),
                  _resident(dm.shape), _resident(bd.shape), _resident(pool_w.shape),
                  _resident((1, POOL_WIDTH)), _resident((1, R_V_WIDTH))],
        out_specs=pl.BlockSpec((1, CHUNK, POOL_WIDTH + R_V_WIDTH), lambda b, c: (b, c, 0)),
        out_shape=jax.ShapeDtypeStruct((bsz, s, POOL_WIDTH + R_V_WIDTH), BF16),
        scratch_shapes=[pltpu.VMEM((POOL_HALO, POOL_WIDTH), F32),
                        pltpu.VMEM((R_QK_WIDTH, R_V_WIDTH), F32)],
        compiler_params=pltpu.CompilerParams(dimension_semantics=("parallel", "arbitrary"),
                                             vmem_limit_bytes=VMEM_LIMIT),
        name="pool_retention",
    )(z, cos_t, sin_t, swap, dq, dk, dc, dm, bd, pool_w.astype(BF16), pool_scale.reshape(1, -1).astype(F32),
      ret_g.reshape(1, -1).astype(F32))


def _ffn_kernel(h_ref, a1_ref, a2_ref, wo_ref, g_ref, wup_ref, cw_ref, cb_ref, wd_ref, gout_ref, o_ref,
                hist_ref, y_ref, x_ref, *, n_tiles, final_norm):
    tm = h_ref.shape[1]
    lanes = x_ref.shape[2]
    n_lane_tiles = FF_TILE // lanes

    @pl.when(pl.program_id(1) == 0)
    def _():
        hist_ref[...] = jnp.zeros_like(hist_ref)

    half = a1_ref.shape[2]
    h1 = h_ref[0] + _dot(a1_ref[0], wo_ref[0:half, :]) + _dot(a2_ref[0], wo_ref[half:, :])
    y_ref[...] = _rms(h1, g_ref[...]).astype(BF16)
    o_ref[0] = h1

    def cols(idx):
        return pl.ds(pl.multiple_of(idx * FF_TILE, FF_TILE), FF_TILE)

    def conv(u, idx, slot):
        w = cw_ref[:, cols(idx)]
        outs = []
        for k in range(n_lane_tiles):
            ls = slice(k * lanes, (k + 1) * lanes)
            xr = x_ref.at[slot * n_lane_tiles + k]
            xr[0:HALO, :] = hist_ref[idx, :, ls]
            xr[HALO:, :] = u[:, ls]
            hist_ref[idx, :, ls] = u[tm - HALO:, ls]
            outs.append(w[2:3, ls] * u[:, ls] + w[1:2, ls] * xr[HALO - 1:HALO - 1 + tm, :]
                        + w[0:1, ls] * xr[HALO - 2:HALO - 2 + tm, :])
        return jnp.concatenate(outs, axis=1) + cb_ref[:, cols(idx)]

    def tile(c, _):
        y = y_ref[...]
        slot = 2 * lax.rem(c, 2)
        gate = conv(_dot(y, wup_ref[:, cols(c)]), c, slot)
        val = conv(_dot(y, wup_ref[:, cols(n_tiles + c)]), n_tiles + c, slot + 1)
        act = (gate * jax.nn.sigmoid(gate) * val).astype(BF16)
        o_ref[0] += _dot(act, wd_ref[cols(c), :])
        return 0

    lax.fori_loop(0, n_tiles, tile, 0, unroll=2)
    if final_norm:
        o_ref[0] = _rms(o_ref[0], gout_ref[...])


def _ffn(h, a1, a2, col1, col2, w_out, norm_g, w_up, conv_w, conv_b, w_down, out_g, final_norm):
    bsz, s, d = h.shape
    tm = min(FFN_ROWS, s)
    half = w_out.shape[0] // 2
    n_tiles = D_FF // FF_TILE
    wo = w_out.astype(BF16)
    wup = w_up.astype(BF16)
    cw = conv_w.astype(F32)
    cb = conv_b.astype(F32).reshape(1, -1)
    wd = w_down.astype(BF16)
    return pl.pallas_call(
        functools.partial(_ffn_kernel, n_tiles=n_tiles, final_norm=final_norm),
        grid=(bsz, s // tm),
        in_specs=[pl.BlockSpec((1, tm, d), lambda b, t: (b, t, 0)),
                  pl.BlockSpec((1, tm, half), lambda b, t: (b, t, col1)),
                  pl.BlockSpec((1, tm, half), lambda b, t: (b, t, col2)),
                  _resident((2 * half, d)), _resident((1, d)),
                  _resident(wup.shape), _resident(cw.shape), _resident(cb.shape), _resident(wd.shape),
                  _resident((1, d))],
        out_specs=pl.BlockSpec((1, tm, d), lambda b, t: (b, t, 0)),
        out_shape=jax.ShapeDtypeStruct((bsz, s, d), F32),
        scratch_shapes=[pltpu.VMEM((2 * n_tiles, HALO, FF_TILE), F32),
                        pltpu.VMEM((tm, d), BF16),
                        pltpu.VMEM((4 * FF_TILE // 128, HALO + tm, 128), F32)],
        compiler_params=pltpu.CompilerParams(dimension_semantics=("parallel", "arbitrary"),
                                             vmem_limit_bytes=VMEM_LIMIT),
        name="out_proj_ffn",
    )(h, a1, a2, wo, norm_g.reshape(1, d).astype(F32), wup, cw, cb, wd, out_g.reshape(1, d).astype(F32))


def kernel(x, rel_bias, norm_mix_g, norm_ffn_g, norm_out_g, w_in_ab, w_out_ab, mlstm_conv_w, mlstm_conv_b, mlstm_b_i,
           mlstm_b_f, mlstm_norm_g, w_in_cd, w_out_cd, pool_w, pool_scale, ret_norm_g, ffn_w_up, ffn_conv_w,
           ffn_conv_b, ffn_w_down):
    bsz, s, d = x.shape
    m = bsz * s

    n_attn = 3 * A_WIDTH
    n_main = n_attn + 4 * B_WIDTH
    n_gate = 128
    w0 = jnp.pad(w_in_ab[0].astype(BF16), ((0, 0), (0, n_main + n_gate - w_in_ab.shape[2])))
    zqkv, zb, zg = _norm_proj(x.reshape(m, d), norm_mix_g[0], w0, [n_attn, n_main - n_attn, n_gate],
                              [BF16, BF16, F32])
    bias_tiles = _bias_tables(rel_bias)
    far_bias = rel_bias[REL_BUCKETS - 1].astype(F32)
    ya = _moba(zqkv.reshape(bsz, s, n_attn), bias_tiles, far_bias)
    yb = _mlstm(zb.reshape(bsz, s, 4 * B_WIDTH), zg, mlstm_conv_w[0], mlstm_conv_b[0], mlstm_b_i[0], mlstm_b_f[0],
                mlstm_norm_g[0])
    h = _ffn(x, ya, yb, 0, 0, w_out_ab[0], norm_ffn_g[0], ffn_w_up[0], ffn_conv_w[0], ffn_conv_b[0], ffn_w_down[0],
             norm_out_g, False)

    (z1,) = _norm_proj(h.reshape(m, d), norm_mix_g[1], w_in_cd[0].astype(BF16), [w_in_cd.shape[2]], [BF16])
    cat = _pool_ret(z1.reshape(bsz, s, -1), pool_w[0], pool_scale[0], ret_norm_g[0])
    return _ffn(h, cat, cat, 0, 1, w_out_cd[0], norm_ffn_g[1], ffn_w_up[1], ffn_conv_w[1], ffn_conv_b[1],
                ffn_w_down[1], norm_out_g, True)
```

```python
import functools
import math

import numpy as np
import jax
import jax.numpy as jnp
from jax import lax
from jax.experimental import pallas as pl
from jax.experimental.pallas import tpu as pltpu

F32 = jnp.float32
BF16 = jnp.bfloat16
I32 = jnp.int32

EPS = 1e-6
NEG = -1e30

D_MODEL = 1024
A_HEADS = 8
A_HEAD_DIM = 64
A_WIDTH = 512
MOBA_BLOCK = 256
MOBA_TOPK = 3
REL_BUCKETS = 32
REL_MAX_DIST = 128
B_HEADS = 4
B_HEAD_DIM = 128
B_WIDTH = 512
MLSTM_CONV = 4
POOL_WINDOWS = (2, 4, 8, 16)
POOL_GROUP_DIM = 128
POOL_WIDTH = 512
R_HEADS = 4
R_QK_DIM = 64
R_V_DIM = 128
R_QK_WIDTH = 256
R_V_WIDTH = 512
ROPE_BASE = 10000.0
D_FF = 2816
FFN_CONV = 3

CHUNK = 256
ONES_ROWS = 16
HALO = 8
POOL_HALO = 16
FF_TILE = 256
PROJ_ROWS = 512
FFN_ROWS = 1024
VMEM_LIMIT = 60 * 1024 * 1024

_NT = (((1,), (1,)), ((), ()))


def _dot(a, b):
    return jnp.dot(a, b, preferred_element_type=F32)


def _dot_nt(a, b):
    return lax.dot_general(a, b, _NT, preferred_element_type=F32)


def _split3(a):
    a1 = a.astype(BF16)
    r = a - a1.astype(F32)
    a2 = r.astype(BF16)
    a3 = (r - a2.astype(F32)).astype(BF16)
    return a1, a2, a3


def _rms(x, g):
    return x * lax.rsqrt(jnp.mean(x * x, axis=-1, keepdims=True) + EPS) * g


def _resident(shape):
    n = len(shape)
    return pl.BlockSpec(shape, lambda *_: (0,) * n, pipeline_mode=pl.Buffered(1))


def _bucket_thresholds():
    max_exact = REL_BUCKETS // 2
    d = np.arange(1, 4 * REL_MAX_DIST, dtype=np.float32)
    large = max_exact + (np.log(d / np.float32(max_exact)) / np.float32(math.log(REL_MAX_DIST / max_exact))
                         * np.float32(REL_BUCKETS - max_exact)).astype(np.int32)
    large = np.minimum(large, REL_BUCKETS - 1)
    return [int(d[np.argmax(large >= b)]) for b in range(max_exact + 1, REL_BUCKETS)]


def _bias_kernel(tab_ref, o_ref, *, thresholds):
    h = pl.program_id(0)
    row = lax.broadcasted_iota(I32, (MOBA_BLOCK, MOBA_BLOCK), 0)
    col = lax.broadcasted_iota(I32, (MOBA_BLOCK, MOBA_BLOCK), 1)
    max_exact = REL_BUCKETS // 2
    for t, off in enumerate((0, MOBA_BLOCK)):
        dist = col - row + off
        d = jnp.maximum(dist, 0)
        large = jnp.full(d.shape, max_exact, I32)
        for thr in thresholds:
            large = large + jnp.where(d >= thr, 1, 0)
        bucket = jnp.where(d < max_exact, d, large)
        val = jnp.zeros(d.shape, F32)
        for i in range(REL_BUCKETS):
            val = jnp.where(bucket == i, tab_ref[h, i], val)
        if off == 0:
            val = jnp.where(dist >= 0, val, NEG)
        o_ref[0, t] = val


def _bias_tables(rel_bias):
    tab = rel_bias.astype(F32).T
    return pl.pallas_call(
        functools.partial(_bias_kernel, thresholds=_bucket_thresholds()),
        grid=(A_HEADS,),
        in_specs=[pl.BlockSpec(memory_space=pltpu.SMEM)],
        out_specs=pl.BlockSpec((1, 2, MOBA_BLOCK, MOBA_BLOCK), lambda h: (h, 0, 0, 0)),
        out_shape=jax.ShapeDtypeStruct((A_HEADS, 2, MOBA_BLOCK, MOBA_BLOCK), F32),
        name="rel_bias_tiles",
    )(tab)


def _norm_proj_kernel(h_ref, g_ref, w_ref, *o_refs, col_tile):
    y = _rms(h_ref[...], g_ref[...]).astype(BF16)
    base = 0
    for o_ref in o_refs:
        n = o_ref.shape[1]
        for c0 in range(0, n, col_tile):
            c1 = min(c0 + col_tile, n)
            o_ref[:, c0:c1] = _dot(y, w_ref[:, base + c0:base + c1]).astype(o_ref.dtype)
        base += n


def _norm_proj(h2d, g, w, out_widths, out_dtypes):
    m, d = h2d.shape
    tm = min(PROJ_ROWS, m)
    assert sum(out_widths) == w.shape[1]
    return pl.pallas_call(
        functools.partial(_norm_proj_kernel, col_tile=512),
        grid=(m // tm,),
        in_specs=[pl.BlockSpec((tm, d), lambda i: (i, 0)), _resident((1, d)), _resident(w.shape)],
        out_specs=[pl.BlockSpec((tm, n), lambda i: (i, 0)) for n in out_widths],
        out_shape=[jax.ShapeDtypeStruct((m, n), dt) for n, dt in zip(out_widths, out_dtypes)],
        compiler_params=pltpu.CompilerParams(dimension_semantics=("parallel",), vmem_limit_bytes=VMEM_LIMIT),
        name="norm_in_proj",
    )(h2d, g.reshape(1, d).astype(F32), w)


def _moba_kernel(far_ref, q_ref, k_ref, v_ref, bias_ref, o_ref, kmean_ref, vt_ref, addm_ref, s_ref, *, nb):
    blk = MOBA_BLOCK
    hd = A_HEAD_DIM
    hp = pl.program_id(1)
    qi = pl.program_id(2)

    @pl.when(qi == 0)
    def _():
        for j in range(nb):
            kb = k_ref[0, j * blk:(j + 1) * blk, :].astype(F32)
            kmean_ref[j:j + 1, :] = jnp.sum(kb, axis=0, keepdims=True) * (1.0 / blk)
            vt = v_ref[0, j * blk:(j + 1) * blk, :].astype(F32).T.astype(BF16)
            for hh in range(2):
                vt_ref[j, hh, 0:hd, :] = vt[hh * hd:(hh + 1) * hd, :]
                vt_ref[j, hh, hd:, :] = jnp.ones((ONES_ROWS, blk), BF16)
        addm_ref[:, nb:, :] = jnp.full((2, HALO, blk), NEG, F32)

    q = q_ref[0]
    lane = lax.broadcasted_iota(I32, (1, 2 * hd), 1)
    rowj = lax.broadcasted_iota(I32, (nb, blk), 0)
    scale = hd ** -0.5
    qms = []
    for hh in range(2):
        in_head = (lane >= hh * hd) & (lane < (hh + 1) * hd)
        qm = jnp.where(in_head, q, jnp.zeros_like(q)) * scale
        qms.append(qm)
        km = jnp.where(in_head, kmean_ref[...], 0.0)
        gate = sum(_dot_nt(part, qm) for part in _split3(km))
        gate = jnp.where(rowj < qi, gate, -jnp.inf)
        ranks = []
        for j in range(nb):
            gj = gate[j:j + 1, :]
            ahead = jnp.where(gate > gj, 1.0, 0.0)
            if j > 0:
                ahead = ahead + jnp.where(rowj < j, jnp.where(gate == gj, 1.0, 0.0), 0.0)
            ranks.append(jnp.sum(ahead, axis=0, keepdims=True))
        rank = jnp.concatenate(ranks, axis=0)
        chosen_bias = jnp.where(rowj == qi - 1, 0.0, far_ref[hp * 2 + hh])
        addm_ref[hh, 0:nb, :] = jnp.where(rowj < qi, jnp.where(rank < float(MOBA_TOPK), chosen_bias, NEG), NEG)

    def k_rows(start, rows):
        return k_ref[0, pl.ds(pl.multiple_of(start, blk), rows), :]

    def item(i, slot, carry):
        t = jnp.minimum(i, nb // 2 - 1)
        j0, j1 = 2 * t, 2 * t + 1
        r1 = jnp.where(j1 <= qi - 2, j1, nb)
        kb = k_rows(j0 * blk, 2 * blk)
        ss = [_dot_nt(kb, qms[hh]) for hh in range(2)]
        blocks = (jnp.where(i == 0, pc, 2 * i - 2), jnp.where(i == 0, qi, 2 * i - 1))
        scores = [s_ref[slot, hh] for hh in range(2)]
        m_new = [jnp.maximum(carry[2 * hh], jnp.max(scores[hh], axis=0, keepdims=True)) for hh in range(2)]
        ps = [jnp.exp(scores[hh] - m_new[hh]).astype(BF16) for hh in range(2)]
        pv = [sum(_dot(vt_ref[j, hh], ps[hh][n * blk:(n + 1) * blk, :]) for n, j in enumerate(blocks))
              for hh in range(2)]
        out = ()
        for hh in range(2):
            alpha = jnp.exp(carry[2 * hh] - m_new[hh])
            out += (m_new[hh], alpha * carry[2 * hh + 1] + pv[hh])
        for hh in range(2):
            s_ref[1 - slot, hh, 0:blk, :] = ss[hh][:blk] + addm_ref[hh, pl.ds(j0, 1), :]
            s_ref[1 - slot, hh, blk:, :] = ss[hh][blk:] + addm_ref[hh, pl.ds(r1, 1), :]
        return out

    pc = jnp.maximum(qi - 1, 0)
    k_prev, k_own = k_rows(pc * blk, blk), k_rows(qi * blk, blk)
    for hh in range(2):
        s_ref[0, hh, 0:blk, :] = _dot_nt(k_prev, qms[hh]) + bias_ref[hh, 1] + addm_ref[hh, pl.ds(pc, 1), :]
        s_ref[0, hh, blk:, :] = _dot_nt(k_own, qms[hh]) + bias_ref[hh, 0]
    n_items = jnp.maximum(qi, 1) // 2 + 1

    start = (jnp.full((1, blk), -jnp.inf, F32), jnp.zeros((hd + ONES_ROWS, blk), F32))
    carry = lax.fori_loop(0, n_items // 2, lambda t, c: item(2 * t + 1, 1, item(2 * t, 0, c)), start + start)
    carry = lax.cond(lax.rem(n_items, 2) == 1, lambda c: item(n_items - 1, 0, c), lambda c: c, carry)
    outs = [carry[2 * hh + 1][:hd] / carry[2 * hh + 1][hd:hd + 1] for hh in range(2)]
    o_ref[0] = jnp.concatenate(outs, axis=0).T.astype(o_ref.dtype)


def _moba(zqkv, bias_tiles, far_bias):
    bsz, s, _ = zqkv.shape
    nb = s // MOBA_BLOCK
    n_pairs = A_HEADS // 2
    pair_w = 2 * A_HEAD_DIM
    return pl.pallas_call(
        functools.partial(_moba_kernel, nb=nb),
        grid=(bsz, n_pairs, nb),
        in_specs=[pl.BlockSpec(memory_space=pltpu.SMEM),
                  pl.BlockSpec((1, MOBA_BLOCK, pair_w), lambda b, hp, qi: (b, qi, hp)),
                  pl.BlockSpec((1, s, pair_w), lambda b, hp, qi: (b, 0, n_pairs + hp)),
                  pl.BlockSpec((1, s, pair_w), lambda b, hp, qi: (b, 0, 2 * n_pairs + hp)),
                  pl.BlockSpec((2, 2, MOBA_BLOCK, MOBA_BLOCK), lambda b, hp, qi: (hp, 0, 0, 0))],
        out_specs=pl.BlockSpec((1, MOBA_BLOCK, pair_w), lambda b, hp, qi: (b, qi, hp)),
        out_shape=jax.ShapeDtypeStruct((bsz, s, A_WIDTH), BF16),
        scratch_shapes=[pltpu.VMEM((nb, pair_w), F32),
                        pltpu.VMEM((nb, 2, A_HEAD_DIM + ONES_ROWS, MOBA_BLOCK), BF16),
                        pltpu.VMEM((2, nb + HALO, MOBA_BLOCK), F32),
                        pltpu.VMEM((2, 2, 2 * MOBA_BLOCK, MOBA_BLOCK), F32)],
        compiler_params=pltpu.CompilerParams(dimension_semantics=("parallel", "parallel", "arbitrary"),
                                             vmem_limit_bytes=VMEM_LIMIT),
        name="moba_attention",
    )(far_bias, zqkv, zqkv, zqkv, bias_tiles)


def _mlstm_kernel(zb_ref, zg_ref, cw_ref, cb_ref, gb_ref, ng_ref, o_ref, hist_ref, state_ref, m_ref, x_ref):
    L = CHUNK
    dh = B_HEAD_DIM

    @pl.when(pl.program_id(1) == 0)
    def _():
        hist_ref[...] = jnp.zeros_like(hist_ref)
        state_ref[...] = jnp.zeros_like(state_ref)
        m_ref[...] = jnp.zeros_like(m_ref)

    x = zb_ref[0, :, 0:2 * B_WIDTH].astype(F32)
    lanes = x_ref.shape[2]
    ys = []
    for k in range(x_ref.shape[0]):
        ls = slice(k * lanes, (k + 1) * lanes)
        x_ref[k, 0:HALO, :] = hist_ref[:, ls]
        x_ref[k, HALO:, :] = x[:, ls]
        yk = cw_ref[MLSTM_CONV - 1:MLSTM_CONV, ls] * x[:, ls]
        for back in range(1, MLSTM_CONV):
            yk = yk + cw_ref[MLSTM_CONV - 1 - back:MLSTM_CONV - back, ls] * x_ref[k, HALO - back:HALO - back + L, :]
        ys.append(yk)
    hist_ref[...] = x[L - HALO:, :]
    y = jnp.concatenate(ys, axis=1) + cb_ref[...]
    y = y * jax.nn.sigmoid(y)
    q_all = y[:, :B_WIDTH]
    k_all = y[:, B_WIDTH:] * (dh ** -0.5)

    g = zg_ref[0] + gb_ref[...]
    lane = lax.broadcasted_iota(I32, (1, 128), 1)
    log_f = jnp.minimum(g, 0.0) - jnp.log(1.0 + jnp.exp(-jnp.abs(g)))
    gates = jnp.where(lane < B_HEADS, g, log_f)
    row = lax.broadcasted_iota(I32, (L, L), 0)
    col = lax.broadcasted_iota(I32, (L, L), 1)
    causal = row >= col
    tril = jnp.where(causal, 1.0, 0.0).astype(BF16)
    triu = jnp.where(row <= col, 1.0, 0.0).astype(BF16)
    csum_col = sum(_dot(tril, part) for part in _split3(gates))
    gates_t = gates.T
    csum_row = sum(_dot(part, triu) for part in _split3(gates_t[0:16, :]))
    unit = jnp.where(lax.broadcasted_iota(I32, (L, dh), 1) == 0, 1.0, 0.0).astype(BF16)

    heads = range(B_HEADS)
    sls = [slice(h * dh, (h + 1) * dh) for h in heads]
    qhs = [q_all[:, sls[h]].astype(BF16) for h in heads]
    khs = [k_all[:, sls[h]] for h in heads]
    states = [state_ref[h] for h in heads]
    m_prevs = [m_ref[h] for h in heads]
    qk_raw = [_dot_nt(qhs[h], khs[h].astype(BF16)) for h in heads]
    inter = [_dot(qhs[h], states[h].astype(BF16)) for h in heads]
    v_augs = [jnp.concatenate([zb_ref[0, :, 2 * B_WIDTH + h * dh:2 * B_WIDTH + (h + 1) * dh], unit], axis=1)
              for h in heads]

    b_cols = [csum_col[:, B_HEADS + h:B_HEADS + h + 1] for h in heads]
    b_rows = [csum_row[B_HEADS + h:B_HEADS + h + 1, :] for h in heads]
    i_rows = [gates_t[h:h + 1, :] for h in heads]
    log_inter = [b_cols[h] + m_prevs[h] for h in heads]
    log_intra = [jnp.where(causal, b_cols[h] - b_rows[h] + i_rows[h], -jnp.inf) for h in heads]
    m_t = [jnp.maximum(log_inter[h], jnp.max(log_intra[h], axis=1, keepdims=True)) for h in heads]
    decay = [jnp.exp(log_intra[h] - m_t[h]) for h in heads]
    w_inter = [jnp.exp(log_inter[h] - m_t[h]) for h in heads]

    nd = [w_inter[h] * inter[h] + _dot((qk_raw[h] * decay[h]).astype(BF16), v_augs[h]) for h in heads]

    b_last = [b_rows[h][:, L - 1:L] for h in heads]
    log_s = [b_last[h] - b_rows[h] + i_rows[h] for h in heads]
    m_new = [jnp.maximum(b_last[h] + m_prevs[h], jnp.max(log_s[h], axis=1, keepdims=True)) for h in heads]
    w_prev = [jnp.exp(b_last[h] + m_prevs[h] - m_new[h]) for h in heads]
    w_s = [jnp.exp(log_s[h] - m_new[h]) for h in heads]
    for h in heads:
        state_ref[h] = w_prev[h] * states[h] + _dot((khs[h].T * w_s[h]).astype(BF16), v_augs[h])
        m_ref[h] = m_new[h]

    for h in heads:
        den = nd[h][:, dh:dh + 1]
        hout = nd[h][:, :dh] / jnp.maximum(jnp.abs(den), jnp.exp(-m_t[h]))
        hn = _rms(hout, ng_ref[:, sls[h]])
        og = jax.nn.sigmoid(zb_ref[0, :, 3 * B_WIDTH + h * dh:3 * B_WIDTH + (h + 1) * dh].astype(F32))
        o_ref[0, :, sls[h]] = (hn * og).astype(o_ref.dtype)


def _mlstm(zb, zg, conv_w, conv_b, b_i, b_f, norm_g):
    bsz, s, _ = zb.shape
    gate_bias = jnp.zeros((1, 128), F32).at[0, :B_HEADS].set(b_i.astype(F32)).at[0, B_HEADS:2 * B_HEADS].set(
        b_f.astype(F32))
    return pl.pallas_call(
        _mlstm_kernel,
        grid=(bsz, s // CHUNK),
        in_specs=[pl.BlockSpec((1, CHUNK, 4 * B_WIDTH), lambda b, c: (b, c, 0)),
                  pl.BlockSpec((1, CHUNK, 128), lambda b, c: (b, c, 0)),
                  _resident((MLSTM_CONV, 2 * B_WIDTH)), _resident((1, 2 * B_WIDTH)),
                  _resident((1, 128)), _resident((1, B_WIDTH))],
        out_specs=pl.BlockSpec((1, CHUNK, B_WIDTH), lambda b, c: (b, c, 0)),
        out_shape=jax.ShapeDtypeStruct((bsz, s, B_WIDTH), BF16),
        scratch_shapes=[pltpu.VMEM((HALO, 2 * B_WIDTH), F32),
                        pltpu.VMEM((B_HEADS, B_HEAD_DIM, 2 * B_HEAD_DIM), F32),
                        pltpu.VMEM((B_HEADS, 1, 1), F32),
                        pltpu.VMEM((2 * B_WIDTH // 128, HALO + CHUNK, 128), F32)],
        compiler_params=pltpu.CompilerParams(dimension_semantics=("parallel", "arbitrary"),
                                             vmem_limit_bytes=VMEM_LIMIT),
        name="mlstm",
    )(zb, zg.reshape(bsz, s, 128), conv_w.astype(F32), conv_b.reshape(1, -1).astype(F32), gate_bias,
      norm_g.reshape(1, -1).astype(F32))


def _pool_ret_kernel(z_ref, cos_ref, sin_ref, swap_ref, dq_ref, dk_ref, dc_ref, dm_ref, bd_ref, pw_ref, ps_ref,
                     rg_ref, o_ref, hist_ref, state_ref, win_ref):
    L = CHUNK
    c = pl.program_id(1)

    @pl.when(c == 0)
    def _():
        hist_ref[...] = jnp.zeros_like(hist_ref)
        state_ref[...] = jnp.zeros_like(state_ref)

    u = z_ref[0, :, 0:POOL_WIDTH].astype(F32)
    pos1 = (c * L + lax.broadcasted_iota(I32, (L, 1), 0) + 1).astype(F32)
    gd = POOL_GROUP_DIM
    for g, w in enumerate(POOL_WINDOWS):
        sl = slice(g * gd, (g + 1) * gd)
        w_ref = win_ref.at[g]
        w_ref[0:POOL_HALO, :] = hist_ref[:, sl]
        w_ref[POOL_HALO:, :] = u[:, sl]
        span = 1
        while span < w:
            rows = POOL_HALO + L - span
            w_ref[span:, :] = w_ref[span:span + rows, :] + w_ref[0:rows, :]
            span *= 2
        pooled = w_ref[POOL_HALO:, :] / jnp.minimum(pos1, float(w)) - u[:, sl]
        o_ref[0, :, sl] = (_dot(pooled.astype(BF16), pw_ref[g]) * ps_ref[:, sl]).astype(o_ref.dtype)

    rq = z_ref[0, :, POOL_WIDTH:POOL_WIDTH + R_QK_WIDTH]
    rk = z_ref[0, :, POOL_WIDTH + R_QK_WIDTH:POOL_WIDTH + 2 * R_QK_WIDTH]
    v0 = POOL_WIDTH + 2 * R_QK_WIDTH
    rv = z_ref[0, :, v0:v0 + R_V_WIDTH]
    cosv, sinv = cos_ref[...], sin_ref[...]
    q_rot = rq.astype(F32) * cosv + _dot(rq, swap_ref[...]) * sinv
    k_rot = (rk.astype(F32) * cosv + _dot(rk, swap_ref[...]) * sinv) * (R_QK_DIM ** -0.5)
    qb = q_rot.astype(BF16)
    kb = k_rot.astype(BF16)
    inter = _dot((q_rot * dq_ref[...]).astype(BF16), state_ref[...].astype(BF16))
    lane = lax.broadcasted_iota(I32, (1, R_QK_WIDTH), 1)
    g0 = v0 + R_V_WIDTH
    for h in range(R_HEADS):
        in_head = (lane >= h * R_QK_DIM) & (lane < (h + 1) * R_QK_DIM)
        sc = _dot_nt(jnp.where(in_head, qb, jnp.zeros_like(qb)), kb) * dm_ref[h]
        vs = slice(h * R_V_DIM, (h + 1) * R_V_DIM)
        y = _dot(sc.astype(BF16), rv[:, vs]) + inter[:, vs]
        gate = z_ref[0, :, g0 + h * R_V_DIM:g0 + (h + 1) * R_V_DIM].astype(F32)
        gate = gate * jax.nn.sigmoid(gate)
        o_ref[0, :, POOL_WIDTH + h * R_V_DIM:POOL_WIDTH + (h + 1) * R_V_DIM] = (
            _rms(y, rg_ref[:, vs]) * gate).astype(o_ref.dtype)
    hist_ref[...] = u[L - POOL_HALO:, :]
    upd = _dot((k_rot * dk_ref[...]).T.astype(BF16), rv)
    state_ref[...] = dc_ref[...] * state_ref[...] + upd * bd_ref[...]


def _retention_tables(s):
    L = CHUNK
    half = R_QK_DIM // 2
    inv = ROPE_BASE ** (-jnp.arange(half, dtype=F32) / half)
    ang = jnp.arange(s, dtype=F32)[:, None] * inv[None, :]
    cos, sin = jnp.cos(ang), jnp.sin(ang)
    cos_t = jnp.tile(jnp.concatenate([cos, cos], axis=-1), (1, R_HEADS))
    sin_t = jnp.tile(jnp.concatenate([-sin, sin], axis=-1), (1, R_HEADS))
    j = np.arange(R_QK_WIDTH)
    src = (j // R_QK_DIM) * R_QK_DIM + (j % R_QK_DIM + half) % R_QK_DIM
    swap = np.zeros((R_QK_WIDTH, R_QK_WIDTH), np.float32)
    swap[src, j] = 1.0
    log_g = jnp.log(1.0 - jnp.exp2(-5.0 - jnp.arange(R_HEADS, dtype=F32)))
    t = jnp.arange(L, dtype=F32)
    rel = t[:, None] - t[None, :]
    decay_intra = jnp.where(rel >= 0, jnp.exp(jnp.maximum(rel, 0.0) * log_g[:, None, None]), 0.0)
    decay_q = jnp.exp((t + 1.0) * log_g[:, None])
    decay_k = jnp.exp((L - 1.0 - t) * log_g[:, None])
    decay_chunk = jnp.exp(L * log_g)
    dq = jnp.repeat(decay_q.T, R_QK_DIM, axis=1)
    dk = jnp.repeat(decay_k.T, R_QK_DIM, axis=1)
    dc = jnp.repeat(decay_chunk, R_QK_DIM)[:, None]
    bd = (np.arange(R_QK_WIDTH)[:, None] // R_QK_DIM == np.arange(R_V_WIDTH)[None, :] // R_V_DIM)
    return cos_t, sin_t, jnp.asarray(swap, BF16), dq, dk, dc, decay_intra, jnp.asarray(bd, F32)


def _pool_ret(z, pool_w, pool_scale, ret_g):
    bsz, s, width = z.shape
    cos_t, sin_t, swap, dq, dk, dc, dm, bd = _retention_tables(s)
    return pl.pallas_call(
        _pool_ret_kernel,
        grid=(bsz, s // CHUNK),
        in_specs=[pl.BlockSpec((1, CHUNK, width), lambda b, c: (b, c, 0)),
                  pl.BlockSpec((CHUNK, R_QK_WIDTH), lambda b, c: (c, 0)),
                  pl.BlockSpec((CHUNK, R_QK_WIDTH), lambda b, c: (c, 0)),
                  _resident(swap.shape), _resident(dq.shape), _resident(dk.shape), _resident(dc.shape),
                  _resident(dm.shape), _resident(bd.shape), _resident(pool_w.shape),
                  _resident((1, POOL_WIDTH)), _resident((1, R_V_WIDTH))],
        out_specs=pl.BlockSpec((1, CHUNK, POOL_WIDTH + R_V_WIDTH), lambda b, c: (b, c, 0)),
        out_shape=jax.ShapeDtypeStruct((bsz, s, POOL_WIDTH + R_V_WIDTH), BF16),
        scratch_shapes=[pltpu.VMEM((POOL_HALO, POOL_WIDTH), F32),
                        pltpu.VMEM((R_QK_WIDTH, R_V_WIDTH), F32),
                        pltpu.VMEM((len(POOL_WINDOWS), POOL_HALO + CHUNK, POOL_GROUP_DIM), F32)],
        compiler_params=pltpu.CompilerParams(dimension_semantics=("parallel", "arbitrary"),
                                             vmem_limit_bytes=VMEM_LIMIT),
        name="pool_retention",
    )(z, cos_t, sin_t, swap, dq, dk, dc, dm, bd, pool_w.astype(BF16), pool_scale.reshape(1, -1).astype(F32),
      ret_g.reshape(1, -1).astype(F32))


def _ffn_kernel(h_ref, a1_ref, a2_ref, wo_ref, g_ref, wup_ref, cw_ref, cb_ref, wd_ref, gout_ref, o_ref,
                hist_ref, y_ref, x_ref, a_ref, *, n_tiles, final_norm):
    tm = h_ref.shape[1]
    lanes = x_ref.shape[2]
    n_lane_tiles = FF_TILE // lanes

    @pl.when(pl.program_id(1) == 0)
    def _():
        hist_ref[...] = jnp.zeros_like(hist_ref)

    half = a1_ref.shape[2]
    h1 = h_ref[0] + _dot(a1_ref[0], wo_ref[0:half, :]) + _dot(a2_ref[0], wo_ref[half:, :])
    y_ref[...] = _rms(h1, g_ref[...]).astype(BF16)
    o_ref[0] = h1

    def cols(idx):
        return pl.ds(pl.multiple_of(idx * FF_TILE, FF_TILE), FF_TILE)

    def conv(u, idx, slot):
        w = cw_ref[:, cols(idx)]
        outs = []
        for k in range(n_lane_tiles):
            ls = slice(k * lanes, (k + 1) * lanes)
            xr = x_ref.at[slot * n_lane_tiles + k]
            xr[0:HALO, :] = hist_ref[idx, :, ls]
            xr[HALO:, :] = u[:, ls]
            hist_ref[idx, :, ls] = xr[tm:tm + HALO, :]
            outs.append(w[2:3, ls] * xr[HALO:HALO + tm, :] + w[1:2, ls] * xr[HALO - 1:HALO - 1 + tm, :]
                        + w[0:1, ls] * xr[HALO - 2:HALO - 2 + tm, :])
        return jnp.concatenate(outs, axis=1) + cb_ref[:, cols(idx)]

    def tile(c, _):
        y = y_ref[...]
        slot = 2 * lax.rem(c, 2)
        gate = conv(_dot(y, wup_ref[:, cols(c)]), c, slot)
        val = conv(_dot(y, wup_ref[:, cols(n_tiles + c)]), n_tiles + c, slot + 1)
        a_ref[:, cols(c)] = (gate * jax.nn.sigmoid(gate) * val).astype(BF16)
        return 0

    lax.fori_loop(0, n_tiles, tile, 0, unroll=2)
    o_ref[0] += _dot(a_ref[...], wd_ref[...])
    if final_norm:
        o_ref[0] = _rms(o_ref[0], gout_ref[...])


def _ffn(h, a1, a2, col1, col2, w_out, norm_g, w_up, conv_w, conv_b, w_down, out_g, final_norm):
    bsz, s, d = h.shape
    tm = min(FFN_ROWS, s)
    half = w_out.shape[0] // 2
    n_tiles = D_FF // FF_TILE
    wo = w_out.astype(BF16)
    wup = w_up.astype(BF16)
    cw = conv_w.astype(F32)
    cb = conv_b.astype(F32).reshape(1, -1)
    wd = w_down.astype(BF16)
    return pl.pallas_call(
        functools.partial(_ffn_kernel, n_tiles=n_tiles, final_norm=final_norm),
        grid=(bsz, s // tm),
        in_specs=[pl.BlockSpec((1, tm, d), lambda b, t: (b, t, 0)),
                  pl.BlockSpec((1, tm, half), lambda b, t: (b, t, col1)),
                  pl.BlockSpec((1, tm, half), lambda b, t: (b, t, col2)),
                  _resident((2 * half, d)), _resident((1, d)),
                  _resident(wup.shape), _resident(cw.shape), _resident(cb.shape), _resident(wd.shape),
                  _resident((1, d))],
        out_specs=pl.BlockSpec((1, tm, d), lambda b, t: (b, t, 0)),
        out_shape=jax.ShapeDtypeStruct((bsz, s, d), F32),
        scratch_shapes=[pltpu.VMEM((2 * n_tiles, HALO, FF_TILE), F32),
                        pltpu.VMEM((tm, d), BF16),
                        pltpu.VMEM((4 * FF_TILE // 128, HALO + tm, 128), F32),
                        pltpu.VMEM((tm, D_FF), BF16)],
        compiler_params=pltpu.CompilerParams(dimension_semantics=("parallel", "arbitrary"),
                                             vmem_limit_bytes=VMEM_LIMIT),
        name="out_proj_ffn",
    )(h, a1, a2, wo, norm_g.reshape(1, d).astype(F32), wup, cw, cb, wd, out_g.reshape(1, d).astype(F32))


def kernel(x, rel_bias, norm_mix_g, norm_ffn_g, norm_out_g, w_in_ab, w_out_ab, mlstm_conv_w, mlstm_conv_b, mlstm_b_i,
           mlstm_b_f, mlstm_norm_g, w_in_cd, w_out_cd, pool_w, pool_scale, ret_norm_g, ffn_w_up, ffn_conv_w,
           ffn_conv_b, ffn_w_down):
    bsz, s, d = x.shape
    m = bsz * s

    n_attn = 3 * A_WIDTH
    n_main = n_attn + 4 * B_WIDTH
    n_gate = 128
    w0 = jnp.pad(w_in_ab[0].astype(BF16), ((0, 0), (0, n_main + n_gate - w_in_ab.shape[2])))
    zqkv, zb, zg = _norm_proj(x.reshape(m, d), norm_mix_g[0], w0, [n_attn, n_main - n_attn, n_gate],
                              [BF16, BF16, F32])
    bias_tiles = _bias_tables(rel_bias)
    far_bias = rel_bias[REL_BUCKETS - 1].astype(F32)
    ya = _moba(zqkv.reshape(bsz, s, n_attn), bias_tiles, far_bias)
    yb = _mlstm(zb.reshape(bsz, s, 4 * B_WIDTH), zg, mlstm_conv_w[0], mlstm_conv_b[0], mlstm_b_i[0], mlstm_b_f[0],
                mlstm_norm_g[0])
    h = _ffn(x, ya, yb, 0, 0, w_out_ab[0], norm_ffn_g[0], ffn_w_up[0], ffn_conv_w[0], ffn_conv_b[0], ffn_w_down[0],
             norm_out_g, False)

    (z1,) = _norm_proj(h.reshape(m, d), norm_mix_g[1], w_in_cd[0].astype(BF16), [w_in_cd.shape[2]], [BF16])
    cat = _pool_ret(z1.reshape(bsz, s, -1), pool_w[0], pool_scale[0], ret_norm_g[0])
    return _ffn(h, cat, cat, 0, 1, w_out_cd[0], norm_ffn_g[1], ffn_w_up[1], ffn_conv_w[1], ffn_conv_b[1],
                ffn_w_down[1], norm_out_g, True)
```

```python
import functools
import math

import numpy as np
import jax
import jax.numpy as jnp
from jax import lax
from jax.experimental import pallas as pl
from jax.experimental.pallas import tpu as pltpu

F32 = jnp.float32
BF16 = jnp.bfloat16
I32 = jnp.int32

EPS = 1e-6
NEG = -1e30

D_MODEL = 1024
A_HEADS = 8
A_HEAD_DIM = 64
A_WIDTH = 512
MOBA_BLOCK = 256
MOBA_TOPK = 3
REL_BUCKETS = 32
REL_MAX_DIST = 128
B_HEADS = 4
B_HEAD_DIM = 128
B_WIDTH = 512
MLSTM_CONV = 4
POOL_WINDOWS = (2, 4, 8, 16)
POOL_GROUP_DIM = 128
POOL_WIDTH = 512
R_HEADS = 4
R_QK_DIM = 64
R_V_DIM = 128
R_QK_WIDTH = 256
R_V_WIDTH = 512
ROPE_BASE = 10000.0
D_FF = 2816
FFN_CONV = 3

CHUNK = 256
ONES_ROWS = 16
HALO = 8
POOL_HALO = 16
FF_TILE = 256
PROJ_ROWS = 1024
FFN_ROWS = 1024
VMEM_LIMIT = 60 * 1024 * 1024

_NT = (((1,), (1,)), ((), ()))


def _dot(a, b):
    return jnp.dot(a, b, preferred_element_type=F32)


def _dot_nt(a, b):
    return lax.dot_general(a, b, _NT, preferred_element_type=F32)


def _split3(a):
    a1 = a.astype(BF16)
    r = a - a1.astype(F32)
    a2 = r.astype(BF16)
    a3 = (r - a2.astype(F32)).astype(BF16)
    return a1, a2, a3


def _rms(x, g):
    return x * lax.rsqrt(jnp.mean(x * x, axis=-1, keepdims=True) + EPS) * g


def _resident(shape):
    n = len(shape)
    return pl.BlockSpec(shape, lambda *_: (0,) * n, pipeline_mode=pl.Buffered(1))


def _bucket_thresholds():
    max_exact = REL_BUCKETS // 2
    d = np.arange(1, 4 * REL_MAX_DIST, dtype=np.float32)
    large = max_exact + (np.log(d / np.float32(max_exact)) / np.float32(math.log(REL_MAX_DIST / max_exact))
                         * np.float32(REL_BUCKETS - max_exact)).astype(np.int32)
    large = np.minimum(large, REL_BUCKETS - 1)
    return [int(d[np.argmax(large >= b)]) for b in range(max_exact + 1, REL_BUCKETS)]


def _bias_kernel(tab_ref, o_ref, *, thresholds):
    h = pl.program_id(0)
    row = lax.broadcasted_iota(I32, (MOBA_BLOCK, MOBA_BLOCK), 0)
    col = lax.broadcasted_iota(I32, (MOBA_BLOCK, MOBA_BLOCK), 1)
    max_exact = REL_BUCKETS // 2
    for t, off in enumerate((0, MOBA_BLOCK)):
        dist = col - row + off
        d = jnp.maximum(dist, 0)
        large = jnp.full(d.shape, max_exact, I32)
        for thr in thresholds:
            large = large + jnp.where(d >= thr, 1, 0)
        bucket = jnp.where(d < max_exact, d, large)
        val = jnp.zeros(d.shape, F32)
        for i in range(REL_BUCKETS):
            val = jnp.where(bucket == i, tab_ref[h, i], val)
        if off == 0:
            val = jnp.where(dist >= 0, val, NEG)
        o_ref[0, t] = val


def _bias_tables(rel_bias):
    tab = rel_bias.astype(F32).T
    return pl.pallas_call(
        functools.partial(_bias_kernel, thresholds=_bucket_thresholds()),
        grid=(A_HEADS,),
        in_specs=[pl.BlockSpec(memory_space=pltpu.SMEM)],
        out_specs=pl.BlockSpec((1, 2, MOBA_BLOCK, MOBA_BLOCK), lambda h: (h, 0, 0, 0)),
        out_shape=jax.ShapeDtypeStruct((A_HEADS, 2, MOBA_BLOCK, MOBA_BLOCK), F32),
        name="rel_bias_tiles",
    )(tab)


def _norm_proj_kernel(h_ref, g_ref, w_ref, *o_refs, col_tile):
    y = _rms(h_ref[...], g_ref[...]).astype(BF16)
    base = 0
    for o_ref in o_refs:
        n = o_ref.shape[1]
        for c0 in range(0, n, col_tile):
            c1 = min(c0 + col_tile, n)
            o_ref[:, c0:c1] = _dot(y, w_ref[:, base + c0:base + c1]).astype(o_ref.dtype)
        base += n


def _norm_proj(h2d, g, w, out_widths, out_dtypes):
    m, d = h2d.shape
    tm = min(PROJ_ROWS, m)
    assert sum(out_widths) == w.shape[1]
    return pl.pallas_call(
        functools.partial(_norm_proj_kernel, col_tile=512),
        grid=(m // tm,),
        in_specs=[pl.BlockSpec((tm, d), lambda i: (i, 0)), _resident((1, d)), _resident(w.shape)],
        out_specs=[pl.BlockSpec((tm, n), lambda i: (i, 0)) for n in out_widths],
        out_shape=[jax.ShapeDtypeStruct((m, n), dt) for n, dt in zip(out_widths, out_dtypes)],
        compiler_params=pltpu.CompilerParams(dimension_semantics=("parallel",), vmem_limit_bytes=VMEM_LIMIT),
        name="norm_in_proj",
    )(h2d, g.reshape(1, d).astype(F32), w)


def _moba_kernel(far_ref, q_ref, k_ref, v_ref, bias_ref, o_ref, kmean_ref, vt_ref, addm_ref, s_ref, *, nb):
    blk = MOBA_BLOCK
    hd = A_HEAD_DIM
    hp = pl.program_id(1)
    qi = pl.program_id(2)

    @pl.when(qi == 0)
    def _():
        for j in range(nb):
            kb = k_ref[0, j * blk:(j + 1) * blk, :].astype(F32)
            kmean_ref[j:j + 1, :] = jnp.sum(kb, axis=0, keepdims=True) * (1.0 / blk)
            vt = v_ref[0, j * blk:(j + 1) * blk, :].astype(F32).T.astype(BF16)
            for hh in range(2):
                vt_ref[j, hh, 0:hd, :] = vt[hh * hd:(hh + 1) * hd, :]
                vt_ref[j, hh, hd:, :] = jnp.ones((ONES_ROWS, blk), BF16)
        addm_ref[:, nb:, :] = jnp.full((2, HALO, blk), NEG, F32)

    q = q_ref[0]
    lane = lax.broadcasted_iota(I32, (1, 2 * hd), 1)
    rowj = lax.broadcasted_iota(I32, (nb, blk), 0)
    scale = hd ** -0.5
    qms = []
    for hh in range(2):
        in_head = (lane >= hh * hd) & (lane < (hh + 1) * hd)
        qm = jnp.where(in_head, q, jnp.zeros_like(q)) * scale
        qms.append(qm)
        km = jnp.where(in_head, kmean_ref[...], 0.0)
        gate = sum(_dot_nt(part, qm) for part in _split3(km))
        gate = jnp.where(rowj < qi, gate, -jnp.inf)
        ranks = []
        for j in range(nb):
            gj = gate[j:j + 1, :]
            ahead = jnp.where(gate > gj, 1.0, 0.0)
            if j > 0:
                ahead = ahead + jnp.where(rowj < j, jnp.where(gate == gj, 1.0, 0.0), 0.0)
            ranks.append(jnp.sum(ahead, axis=0, keepdims=True))
        rank = jnp.concatenate(ranks, axis=0)
        chosen_bias = jnp.where(rowj == qi - 1, 0.0, far_ref[hp * 2 + hh])
        addm_ref[hh, 0:nb, :] = jnp.where(rowj < qi, jnp.where(rank < float(MOBA_TOPK), chosen_bias, NEG), NEG)

    def k_rows(start, rows):
        return k_ref[0, pl.ds(pl.multiple_of(start, blk), rows), :]

    def item(i, slot, carry):
        t = jnp.minimum(i, nb // 2 - 1)
        j0, j1 = 2 * t, 2 * t + 1
        r1 = jnp.where(j1 <= qi - 2, j1, nb)
        kb = k_rows(j0 * blk, 2 * blk)
        ss = [_dot_nt(kb, qms[hh]) for hh in range(2)]
        blocks = (jnp.where(i == 0, pc, 2 * i - 2), jnp.where(i == 0, qi, 2 * i - 1))
        scores = [s_ref[slot, hh] for hh in range(2)]
        m_new = [jnp.maximum(carry[2 * hh], jnp.max(scores[hh], axis=0, keepdims=True)) for hh in range(2)]
        ps = [jnp.exp(scores[hh] - m_new[hh]).astype(BF16) for hh in range(2)]
        pv = [sum(_dot(vt_ref[j, hh], ps[hh][n * blk:(n + 1) * blk, :]) for n, j in enumerate(blocks))
              for hh in range(2)]
        out = ()
        for hh in range(2):
            alpha = jnp.exp(carry[2 * hh] - m_new[hh])
            out += (m_new[hh], alpha * carry[2 * hh + 1] + pv[hh])
        for hh in range(2):
            s_ref[1 - slot, hh, 0:blk, :] = ss[hh][:blk] + addm_ref[hh, pl.ds(j0, 1), :]
            s_ref[1 - slot, hh, blk:, :] = ss[hh][blk:] + addm_ref[hh, pl.ds(r1, 1), :]
        return out

    pc = jnp.maximum(qi - 1, 0)
    k_prev, k_own = k_rows(pc * blk, blk), k_rows(qi * blk, blk)
    for hh in range(2):
        s_ref[0, hh, 0:blk, :] = _dot_nt(k_prev, qms[hh]) + bias_ref[hh, 1] + addm_ref[hh, pl.ds(pc, 1), :]
        s_ref[0, hh, blk:, :] = _dot_nt(k_own, qms[hh]) + bias_ref[hh, 0]
    n_items = jnp.maximum(qi, 1) // 2 + 1

    start = (jnp.full((1, blk), -jnp.inf, F32), jnp.zeros((hd + ONES_ROWS, blk), F32))
    carry = lax.fori_loop(0, n_items // 2, lambda t, c: item(2 * t + 1, 1, item(2 * t, 0, c)), start + start)
    carry = lax.cond(lax.rem(n_items, 2) == 1, lambda c: item(n_items - 1, 0, c), lambda c: c, carry)
    outs = [carry[2 * hh + 1][:hd] / carry[2 * hh + 1][hd:hd + 1] for hh in range(2)]
    o_ref[0] = jnp.concatenate(outs, axis=0).T.astype(o_ref.dtype)


def _moba(zqkv, bias_tiles, far_bias):
    bsz, s, _ = zqkv.shape
    nb = s // MOBA_BLOCK
    n_pairs = A_HEADS // 2
    pair_w = 2 * A_HEAD_DIM
    return pl.pallas_call(
        functools.partial(_moba_kernel, nb=nb),
        grid=(bsz, n_pairs, nb),
        in_specs=[pl.BlockSpec(memory_space=pltpu.SMEM),
                  pl.BlockSpec((1, MOBA_BLOCK, pair_w), lambda b, hp, qi: (b, qi, hp)),
                  pl.BlockSpec((1, s, pair_w), lambda b, hp, qi: (b, 0, n_pairs + hp)),
                  pl.BlockSpec((1, s, pair_w), lambda b, hp, qi: (b, 0, 2 * n_pairs + hp)),
                  pl.BlockSpec((2, 2, MOBA_BLOCK, MOBA_BLOCK), lambda b, hp, qi: (hp, 0, 0, 0))],
        out_specs=pl.BlockSpec((1, MOBA_BLOCK, pair_w), lambda b, hp, qi: (b, qi, hp)),
        out_shape=jax.ShapeDtypeStruct((bsz, s, A_WIDTH), BF16),
        scratch_shapes=[pltpu.VMEM((nb, pair_w), F32),
                        pltpu.VMEM((nb, 2, A_HEAD_DIM + ONES_ROWS, MOBA_BLOCK), BF16),
                        pltpu.VMEM((2, nb + HALO, MOBA_BLOCK), F32),
                        pltpu.VMEM((2, 2, 2 * MOBA_BLOCK, MOBA_BLOCK), F32)],
        compiler_params=pltpu.CompilerParams(dimension_semantics=("parallel", "parallel", "arbitrary"),
                                             vmem_limit_bytes=VMEM_LIMIT),
        name="moba_attention",
    )(far_bias, zqkv, zqkv, zqkv, bias_tiles)


def _mlstm_kernel(zb_ref, zg_ref, cw_ref, cb_ref, gb_ref, ng_ref, o_ref, hist_ref, state_ref, m_ref, x_ref):
    L = CHUNK
    dh = B_HEAD_DIM

    @pl.when(pl.program_id(1) == 0)
    def _():
        hist_ref[...] = jnp.zeros_like(hist_ref)
        state_ref[...] = jnp.zeros_like(state_ref)
        m_ref[...] = jnp.zeros_like(m_ref)

    x = zb_ref[0, :, 0:2 * B_WIDTH].astype(F32)
    lanes = x_ref.shape[2]
    ys = []
    for k in range(x_ref.shape[0]):
        ls = slice(k * lanes, (k + 1) * lanes)
        x_ref[k, 0:HALO, :] = hist_ref[:, ls]
        x_ref[k, HALO:, :] = x[:, ls]
        yk = cw_ref[MLSTM_CONV - 1:MLSTM_CONV, ls] * x[:, ls]
        for back in range(1, MLSTM_CONV):
            yk = yk + cw_ref[MLSTM_CONV - 1 - back:MLSTM_CONV - back, ls] * x_ref[k, HALO - back:HALO - back + L, :]
        ys.append(yk)
    hist_ref[...] = x[L - HALO:, :]
    y = jnp.concatenate(ys, axis=1) + cb_ref[...]
    y = y * jax.nn.sigmoid(y)
    q_all = y[:, :B_WIDTH]
    k_all = y[:, B_WIDTH:] * (dh ** -0.5)

    g = zg_ref[0] + gb_ref[...]
    lane = lax.broadcasted_iota(I32, (1, 128), 1)
    log_f = jnp.minimum(g, 0.0) - jnp.log(1.0 + jnp.exp(-jnp.abs(g)))
    gates = jnp.where(lane < B_HEADS, g, log_f)
    row = lax.broadcasted_iota(I32, (L, L), 0)
    col = lax.broadcasted_iota(I32, (L, L), 1)
    causal = row >= col
    tril = jnp.where(causal, 1.0, 0.0).astype(BF16)
    triu = jnp.where(row <= col, 1.0, 0.0).astype(BF16)
    csum_col = sum(_dot(tril, part) for part in _split3(gates))
    gates_t = gates.T
    csum_row = sum(_dot(part, triu) for part in _split3(gates_t[0:16, :]))
    top_row = lax.broadcasted_iota(I32, (dh, L), 0) == 0

    heads = range(B_HEADS)
    sls = [slice(h * dh, (h + 1) * dh) for h in heads]
    qhs = [q_all[:, sls[h]].astype(BF16) for h in heads]
    khs = [k_all[:, sls[h]].astype(BF16) for h in heads]
    states = [state_ref[h] for h in heads]
    m_prevs = [m_ref[h] for h in heads]
    qk_t = [_dot_nt(khs[h], qhs[h]) for h in heads]
    inter_t = [_dot_nt(states[h].astype(BF16), qhs[h]) for h in heads]
    v_t = [zb_ref[0, :, 2 * B_WIDTH + h * dh:2 * B_WIDTH + (h + 1) * dh].astype(F32).T for h in heads]
    ones_t = jnp.where(top_row, 1.0, 0.0).astype(BF16)
    vaug_t = [jnp.concatenate([v_t[h].astype(BF16), ones_t], axis=0) for h in heads]

    c_cols = [gates[:, h:h + 1] - csum_col[:, B_HEADS + h:B_HEADS + h + 1] for h in heads]
    b_rows = [csum_row[B_HEADS + h:B_HEADS + h + 1, :] for h in heads]
    i_rows = [gates_t[h:h + 1, :] for h in heads]
    log_inter = [b_rows[h] + m_prevs[h] for h in heads]
    log_intra = [jnp.where(row <= col, b_rows[h] + c_cols[h], -jnp.inf) for h in heads]
    m_t = [jnp.maximum(log_inter[h], jnp.max(log_intra[h], axis=0, keepdims=True)) for h in heads]
    decay = [jnp.exp(log_intra[h] - m_t[h]) for h in heads]
    w_inter = [jnp.exp(log_inter[h] - m_t[h]) for h in heads]

    nd = [w_inter[h] * inter_t[h] + _dot(vaug_t[h], (qk_t[h] * decay[h]).astype(BF16)) for h in heads]

    b_last = [b_rows[h][:, L - 1:L] for h in heads]
    log_s = [b_last[h] - b_rows[h] + i_rows[h] for h in heads]
    m_new = [jnp.maximum(b_last[h] + m_prevs[h], jnp.max(log_s[h], axis=1, keepdims=True)) for h in heads]
    w_prev = [jnp.exp(b_last[h] + m_prevs[h] - m_new[h]) for h in heads]
    w_s = [jnp.exp(log_s[h] - m_new[h]) for h in heads]
    for h in heads:
        weighted = jnp.concatenate([v_t[h] * w_s[h], jnp.where(top_row, w_s[h], 0.0)], axis=0).astype(BF16)
        state_ref[h] = w_prev[h] * states[h] + _dot(weighted, khs[h])
        m_ref[h] = m_new[h]

    for h in heads:
        den = nd[h][dh:dh + 1, :]
        hout = nd[h][:dh, :] / jnp.maximum(jnp.abs(den), jnp.exp(-m_t[h]))
        hn = hout * lax.rsqrt(jnp.mean(hout * hout, axis=0, keepdims=True) + EPS) * ng_ref[sls[h], :]
        og = jax.nn.sigmoid(zb_ref[0, :, 3 * B_WIDTH + h * dh:3 * B_WIDTH + (h + 1) * dh].astype(F32))
        o_ref[0, :, sls[h]] = (hn.T * og).astype(o_ref.dtype)


def _mlstm(zb, zg, conv_w, conv_b, b_i, b_f, norm_g):
    bsz, s, _ = zb.shape
    gate_bias = jnp.zeros((1, 128), F32).at[0, :B_HEADS].set(b_i.astype(F32)).at[0, B_HEADS:2 * B_HEADS].set(
        b_f.astype(F32))
    return pl.pallas_call(
        _mlstm_kernel,
        grid=(bsz, s // CHUNK),
        in_specs=[pl.BlockSpec((1, CHUNK, 4 * B_WIDTH), lambda b, c: (b, c, 0)),
                  pl.BlockSpec((1, CHUNK, 128), lambda b, c: (b, c, 0)),
                  _resident((MLSTM_CONV, 2 * B_WIDTH)), _resident((1, 2 * B_WIDTH)),
                  _resident((1, 128)), _resident((B_WIDTH, 1))],
        out_specs=pl.BlockSpec((1, CHUNK, B_WIDTH), lambda b, c: (b, c, 0)),
        out_shape=jax.ShapeDtypeStruct((bsz, s, B_WIDTH), BF16),
        scratch_shapes=[pltpu.VMEM((HALO, 2 * B_WIDTH), F32),
                        pltpu.VMEM((B_HEADS, 2 * B_HEAD_DIM, B_HEAD_DIM), F32),
                        pltpu.VMEM((B_HEADS, 1, 1), F32),
                        pltpu.VMEM((2 * B_WIDTH // 128, HALO + CHUNK, 128), F32)],
        compiler_params=pltpu.CompilerParams(dimension_semantics=("parallel", "arbitrary"),
                                             vmem_limit_bytes=VMEM_LIMIT),
        name="mlstm",
    )(zb, zg.reshape(bsz, s, 128), conv_w.astype(F32), conv_b.reshape(1, -1).astype(F32), gate_bias,
      norm_g.reshape(-1, 1).astype(F32))


def _pool_ret_kernel(z_ref, cos_ref, sin_ref, swap_ref, dq_ref, dk_ref, dc_ref, dm_ref, bd_ref, pw_ref, ps_ref,
                     rg_ref, o_ref, hist_ref, state_ref, win_ref):
    L = CHUNK
    c = pl.program_id(1)

    @pl.when(c == 0)
    def _():
        hist_ref[...] = jnp.zeros_like(hist_ref)
        state_ref[...] = jnp.zeros_like(state_ref)

    u = z_ref[0, :, 0:POOL_WIDTH].astype(F32)
    pos1 = (c * L + lax.broadcasted_iota(I32, (L, 1), 0) + 1).astype(F32)
    gd = POOL_GROUP_DIM
    for g, w in enumerate(POOL_WINDOWS):
        sl = slice(g * gd, (g + 1) * gd)
        w_ref = win_ref.at[g]
        w_ref[0:POOL_HALO, :] = hist_ref[:, sl]
        w_ref[POOL_HALO:, :] = u[:, sl]
        span = 1
        while span < w:
            rows = POOL_HALO + L - span
            w_ref[span:, :] = w_ref[span:span + rows, :] + w_ref[0:rows, :]
            span *= 2
        pooled = w_ref[POOL_HALO:, :] / jnp.minimum(pos1, float(w)) - u[:, sl]
        o_ref[0, :, sl] = (_dot(pooled.astype(BF16), pw_ref[g]) * ps_ref[:, sl]).astype(o_ref.dtype)

    rq = z_ref[0, :, POOL_WIDTH:POOL_WIDTH + R_QK_WIDTH]
    rk = z_ref[0, :, POOL_WIDTH + R_QK_WIDTH:POOL_WIDTH + 2 * R_QK_WIDTH]
    v0 = POOL_WIDTH + 2 * R_QK_WIDTH
    rv = z_ref[0, :, v0:v0 + R_V_WIDTH]
    cosv, sinv = cos_ref[...], sin_ref[...]
    q_rot = rq.astype(F32) * cosv + _dot(rq, swap_ref[...]) * sinv
    k_rot = (rk.astype(F32) * cosv + _dot(rk, swap_ref[...]) * sinv) * (R_QK_DIM ** -0.5)
    qb = q_rot.astype(BF16)
    kb = k_rot.astype(BF16)
    inter = _dot((q_rot * dq_ref[...]).astype(BF16), state_ref[...].astype(BF16))
    lane = lax.broadcasted_iota(I32, (1, R_QK_WIDTH), 1)
    g0 = v0 + R_V_WIDTH
    for h in range(R_HEADS):
        in_head = (lane >= h * R_QK_DIM) & (lane < (h + 1) * R_QK_DIM)
        sc = _dot_nt(jnp.where(in_head, qb, jnp.zeros_like(qb)), kb) * dm_ref[h]
        vs = slice(h * R_V_DIM, (h + 1) * R_V_DIM)
        y = _dot(sc.astype(BF16), rv[:, vs]) + inter[:, vs]
        gate = z_ref[0, :, g0 + h * R_V_DIM:g0 + (h + 1) * R_V_DIM].astype(F32)
        gate = gate * jax.nn.sigmoid(gate)
        o_ref[0, :, POOL_WIDTH + h * R_V_DIM:POOL_WIDTH + (h + 1) * R_V_DIM] = (
            _rms(y, rg_ref[:, vs]) * gate).astype(o_ref.dtype)
    hist_ref[...] = u[L - POOL_HALO:, :]
    upd = _dot((k_rot * dk_ref[...]).T.astype(BF16), rv)
    state_ref[...] = dc_ref[...] * state_ref[...] + upd * bd_ref[...]


def _retention_tables(s):
    L = CHUNK
    half = R_QK_DIM // 2
    inv = ROPE_BASE ** (-jnp.arange(half, dtype=F32) / half)
    ang = jnp.arange(s, dtype=F32)[:, None] * inv[None, :]
    cos, sin = jnp.cos(ang), jnp.sin(ang)
    cos_t = jnp.tile(jnp.concatenate([cos, cos], axis=-1), (1, R_HEADS))
    sin_t = jnp.tile(jnp.concatenate([-sin, sin], axis=-1), (1, R_HEADS))
    j = np.arange(R_QK_WIDTH)
    src = (j // R_QK_DIM) * R_QK_DIM + (j % R_QK_DIM + half) % R_QK_DIM
    swap = np.zeros((R_QK_WIDTH, R_QK_WIDTH), np.float32)
    swap[src, j] = 1.0
    log_g = jnp.log(1.0 - jnp.exp2(-5.0 - jnp.arange(R_HEADS, dtype=F32)))
    t = jnp.arange(L, dtype=F32)
    rel = t[:, None] - t[None, :]
    decay_intra = jnp.where(rel >= 0, jnp.exp(jnp.maximum(rel, 0.0) * log_g[:, None, None]), 0.0)
    decay_q = jnp.exp((t + 1.0) * log_g[:, None])
    decay_k = jnp.exp((L - 1.0 - t) * log_g[:, None])
    decay_chunk = jnp.exp(L * log_g)
    dq = jnp.repeat(decay_q.T, R_QK_DIM, axis=1)
    dk = jnp.repeat(decay_k.T, R_QK_DIM, axis=1)
    dc = jnp.repeat(decay_chunk, R_QK_DIM)[:, None]
    bd = (np.arange(R_QK_WIDTH)[:, None] // R_QK_DIM == np.arange(R_V_WIDTH)[None, :] // R_V_DIM)
    return cos_t, sin_t, jnp.asarray(swap, BF16), dq, dk, dc, decay_intra, jnp.asarray(bd, F32)


def _pool_ret(z, pool_w, pool_scale, ret_g):
    bsz, s, width = z.shape
    cos_t, sin_t, swap, dq, dk, dc, dm, bd = _retention_tables(s)
    return pl.pallas_call(
        _pool_ret_kernel,
        grid=(bsz, s // CHUNK),
        in_specs=[pl.BlockSpec((1, CHUNK, width), lambda b, c: (b, c, 0)),
                  pl.BlockSpec((CHUNK, R_QK_WIDTH), lambda b, c: (c, 0)),
                  pl.BlockSpec((CHUNK, R_QK_WIDTH), lambda b, c: (c, 0)),
                  _resident(swap.shape), _resident(dq.shape), _resident(dk.shape), _resident(dc.shape),
                  _resident(dm.shape), _resident(bd.shape), _resident(pool_w.shape),
                  _resident((1, POOL_WIDTH)), _resident((1, R_V_WIDTH))],
        out_specs=pl.BlockSpec((1, CHUNK, POOL_WIDTH + R_V_WIDTH), lambda b, c: (b, c, 0)),
        out_shape=jax.ShapeDtypeStruct((bsz, s, POOL_WIDTH + R_V_WIDTH), BF16),
        scratch_shapes=[pltpu.VMEM((POOL_HALO, POOL_WIDTH), F32),
                        pltpu.VMEM((R_QK_WIDTH, R_V_WIDTH), F32),
                        pltpu.VMEM((len(POOL_WINDOWS), POOL_HALO + CHUNK, POOL_GROUP_DIM), F32)],
        compiler_params=pltpu.CompilerParams(dimension_semantics=("parallel", "arbitrary"),
                                             vmem_limit_bytes=VMEM_LIMIT),
        name="pool_retention",
    )(z, cos_t, sin_t, swap, dq, dk, dc, dm, bd, pool_w.astype(BF16), pool_scale.reshape(1, -1).astype(F32),
      ret_g.reshape(1, -1).astype(F32))


def _ffn_kernel(h_ref, a1_ref, a2_ref, wo_ref, g_ref, wup_ref, cw_ref, cb_ref, wd_ref, gout_ref, o_ref,
                hist_ref, y_ref, x_ref, a_ref, *, n_tiles, final_norm):
    tm = h_ref.shape[1]
    lanes = x_ref.shape[2]
    n_lane_tiles = FF_TILE // lanes

    @pl.when(pl.program_id(1) == 0)
    def _():
        hist_ref[...] = jnp.zeros_like(hist_ref)

    half = a1_ref.shape[2]
    h1 = h_ref[0] + _dot(a1_ref[0], wo_ref[0:half, :]) + _dot(a2_ref[0], wo_ref[half:, :])
    y_ref[...] = _rms(h1, g_ref[...]).astype(BF16)
    o_ref[0] = h1

    def cols(idx):
        return pl.ds(pl.multiple_of(idx * FF_TILE, FF_TILE), FF_TILE)

    def conv(u, idx, slot):
        w = cw_ref[:, cols(idx)]
        outs = []
        for k in range(n_lane_tiles):
            ls = slice(k * lanes, (k + 1) * lanes)
            xr = x_ref.at[slot * n_lane_tiles + k]
            xr[0:HALO, :] = hist_ref[idx, :, ls]
            xr[HALO:, :] = u[:, ls]
            hist_ref[idx, :, ls] = xr[tm:tm + HALO, :]
            outs.append(w[2:3, ls] * xr[HALO:HALO + tm, :] + w[1:2, ls] * xr[HALO - 1:HALO - 1 + tm, :]
                        + w[0:1, ls] * xr[HALO - 2:HALO - 2 + tm, :])
        return jnp.concatenate(outs, axis=1) + cb_ref[:, cols(idx)]

    def tile(c, _):
        y = y_ref[...]
        slot = 2 * lax.rem(c, 2)
        gate = conv(_dot(y, wup_ref[:, cols(c)]), c, slot)
        val = conv(_dot(y, wup_ref[:, cols(n_tiles + c)]), n_tiles + c, slot + 1)
        a_ref[:, cols(c)] = (gate * jax.nn.sigmoid(gate) * val).astype(BF16)
        return 0

    lax.fori_loop(0, n_tiles, tile, 0, unroll=2)
    o_ref[0] += _dot(a_ref[...], wd_ref[...])
    if final_norm:
        o_ref[0] = _rms(o_ref[0], gout_ref[...])


def _ffn(h, a1, a2, col1, col2, w_out, norm_g, w_up, conv_w, conv_b, w_down, out_g, final_norm):
    bsz, s, d = h.shape
    tm = min(FFN_ROWS, s)
    half = w_out.shape[0] // 2
    n_tiles = D_FF // FF_TILE
    wo = w_out.astype(BF16)
    wup = w_up.astype(BF16)
    cw = conv_w.astype(F32)
    cb = conv_b.astype(F32).reshape(1, -1)
    wd = w_down.astype(BF16)
    return pl.pallas_call(
        functools.partial(_ffn_kernel, n_tiles=n_tiles, final_norm=final_norm),
        grid=(bsz, s // tm),
        in_specs=[pl.BlockSpec((1, tm, d), lambda b, t: (b, t, 0)),
                  pl.BlockSpec((1, tm, half), lambda b, t: (b, t, col1)),
                  pl.BlockSpec((1, tm, half), lambda b, t: (b, t, col2)),
                  _resident((2 * half, d)), _resident((1, d)),
                  _resident(wup.shape), _resident(cw.shape), _resident(cb.shape), _resident(wd.shape),
                  _resident((1, d))],
        out_specs=pl.BlockSpec((1, tm, d), lambda b, t: (b, t, 0)),
        out_shape=jax.ShapeDtypeStruct((bsz, s, d), F32),
        scratch_shapes=[pltpu.VMEM((2 * n_tiles, HALO, FF_TILE), F32),
                        pltpu.VMEM((tm, d), BF16),
                        pltpu.VMEM((4 * FF_TILE // 128, HALO + tm, 128), F32),
                        pltpu.VMEM((tm, D_FF), BF16)],
        compiler_params=pltpu.CompilerParams(dimension_semantics=("parallel", "arbitrary"),
                                             vmem_limit_bytes=VMEM_LIMIT),
        name="out_proj_ffn",
    )(h, a1, a2, wo, norm_g.reshape(1, d).astype(F32), wup, cw, cb, wd, out_g.reshape(1, d).astype(F32))


def kernel(x, rel_bias, norm_mix_g, norm_ffn_g, norm_out_g, w_in_ab, w_out_ab, mlstm_conv_w, mlstm_conv_b, mlstm_b_i,
           mlstm_b_f, mlstm_norm_g, w_in_cd, w_out_cd, pool_w, pool_scale, ret_norm_g, ffn_w_up, ffn_conv_w,
           ffn_conv_b, ffn_w_down):
    bsz, s, d = x.shape
    m = bsz * s

    n_attn = 3 * A_WIDTH
    n_main = n_attn + 4 * B_WIDTH
    n_gate = 128
    w0 = jnp.pad(w_in_ab[0].astype(BF16), ((0, 0), (0, n_main + n_gate - w_in_ab.shape[2])))
    zqkv, zb, zg = _norm_proj(x.reshape(m, d), norm_mix_g[0], w0, [n_attn, n_main - n_attn, n_gate],
                              [BF16, BF16, F32])
    bias_tiles = _bias_tables(rel_bias)
    far_bias = rel_bias[REL_BUCKETS - 1].astype(F32)
    ya = _moba(zqkv.reshape(bsz, s, n_attn), bias_tiles, far_bias)
    yb = _mlstm(zb.reshape(bsz, s, 4 * B_WIDTH), zg, mlstm_conv_w[0], mlstm_conv_b[0], mlstm_b_i[0], mlstm_b_f[0],
                mlstm_norm_g[0])
    h = _ffn(x, ya, yb, 0, 0, w_out_ab[0], norm_ffn_g[0], ffn_w_up[0], ffn_conv_w[0], ffn_conv_b[0], ffn_w_down[0],
             norm_out_g, False)

    (z1,) = _norm_proj(h.reshape(m, d), norm_mix_g[1], w_in_cd[0].astype(BF16), [w_in_cd.shape[2]], [BF16])
    cat = _pool_ret(z1.reshape(bsz, s, -1), pool_w[0], pool_scale[0], ret_norm_g[0])
    return _ffn(h, cat, cat, 0, 1, w_out_cd[0], norm_ffn_g[1], ffn_w_up[1], ffn_conv_w[1], ffn_conv_b[1],
                ffn_w_down[1], norm_out_g, True)
```

```python
import functools
import math

import numpy as np
import jax
import jax.numpy as jnp
from jax import lax
from jax.experimental import pallas as pl
from jax.experimental.pallas import tpu as pltpu

F32 = jnp.float32
BF16 = jnp.bfloat16
I32 = jnp.int32

EPS = 1e-6
NEG = -1e30

D_MODEL = 1024
A_HEADS = 8
A_HEAD_DIM = 64
A_WIDTH = 512
MOBA_BLOCK = 256
MOBA_TOPK = 3
REL_BUCKETS = 32
REL_MAX_DIST = 128
B_HEADS = 4
B_HEAD_DIM = 128
B_WIDTH = 512
MLSTM_CONV = 4
POOL_WINDOWS = (2, 4, 8, 16)
POOL_GROUP_DIM = 128
POOL_WIDTH = 512
R_HEADS = 4
R_QK_DIM = 64
R_V_DIM = 128
R_QK_WIDTH = 256
R_V_WIDTH = 512
ROPE_BASE = 10000.0
D_FF = 2816
FFN_CONV = 3

CHUNK = 256
ONES_ROWS = 16
HALO = 8
POOL_HALO = 16
FF_TILE = 256
PROJ_ROWS = 1024
FFN_ROWS = 1024
MIXER_ROWS = 1024
VMEM_LIMIT = 60 * 1024 * 1024

_NT = (((1,), (1,)), ((), ()))


def _dot(a, b):
    return jnp.dot(a, b, preferred_element_type=F32)


def _dot_nt(a, b):
    return lax.dot_general(a, b, _NT, preferred_element_type=F32)


def _split3(a):
    a1 = a.astype(BF16)
    r = a - a1.astype(F32)
    a2 = r.astype(BF16)
    a3 = (r - a2.astype(F32)).astype(BF16)
    return a1, a2, a3


def _rms(x, g):
    return x * lax.rsqrt(jnp.mean(x * x, axis=-1, keepdims=True) + EPS) * g


def _resident(shape):
    n = len(shape)
    return pl.BlockSpec(shape, lambda *_: (0,) * n, pipeline_mode=pl.Buffered(1))


def _bucket_thresholds():
    max_exact = REL_BUCKETS // 2
    d = np.arange(1, 4 * REL_MAX_DIST, dtype=np.float32)
    large = max_exact + (np.log(d / np.float32(max_exact)) / np.float32(math.log(REL_MAX_DIST / max_exact))
                         * np.float32(REL_BUCKETS - max_exact)).astype(np.int32)
    large = np.minimum(large, REL_BUCKETS - 1)
    return [int(d[np.argmax(large >= b)]) for b in range(max_exact + 1, REL_BUCKETS)]


def _bias_kernel(tab_ref, o_ref, *, thresholds):
    h = pl.program_id(0)
    row = lax.broadcasted_iota(I32, (MOBA_BLOCK, MOBA_BLOCK), 0)
    col = lax.broadcasted_iota(I32, (MOBA_BLOCK, MOBA_BLOCK), 1)
    max_exact = REL_BUCKETS // 2
    for t, off in enumerate((0, MOBA_BLOCK)):
        dist = col - row + off
        d = jnp.maximum(dist, 0)
        large = jnp.full(d.shape, max_exact, I32)
        for thr in thresholds:
            large = large + jnp.where(d >= thr, 1, 0)
        bucket = jnp.where(d < max_exact, d, large)
        val = jnp.zeros(d.shape, F32)
        for i in range(REL_BUCKETS):
            val = jnp.where(bucket == i, tab_ref[h, i], val)
        if off == 0:
            val = jnp.where(dist >= 0, val, NEG)
        o_ref[0, t] = val


def _bias_tables(rel_bias):
    tab = rel_bias.astype(F32).T
    return pl.pallas_call(
        functools.partial(_bias_kernel, thresholds=_bucket_thresholds()),
        grid=(A_HEADS,),
        in_specs=[pl.BlockSpec(memory_space=pltpu.SMEM)],
        out_specs=pl.BlockSpec((1, 2, MOBA_BLOCK, MOBA_BLOCK), lambda h: (h, 0, 0, 0)),
        out_shape=jax.ShapeDtypeStruct((A_HEADS, 2, MOBA_BLOCK, MOBA_BLOCK), F32),
        name="rel_bias_tiles",
    )(tab)


def _norm_proj_kernel(h_ref, g_ref, w_ref, *o_refs, col_tile):
    y = _rms(h_ref[...], g_ref[...]).astype(BF16)
    base = 0
    for o_ref in o_refs:
        n = o_ref.shape[1]
        for c0 in range(0, n, col_tile):
            c1 = min(c0 + col_tile, n)
            o_ref[:, c0:c1] = _dot(y, w_ref[:, base + c0:base + c1]).astype(o_ref.dtype)
        base += n


def _norm_proj(h2d, g, w, out_widths, out_dtypes):
    m, d = h2d.shape
    tm = min(PROJ_ROWS, m)
    assert sum(out_widths) == w.shape[1]
    return pl.pallas_call(
        functools.partial(_norm_proj_kernel, col_tile=512),
        grid=(m // tm,),
        in_specs=[pl.BlockSpec((tm, d), lambda i: (i, 0)), _resident((1, d)), _resident(w.shape)],
        out_specs=[pl.BlockSpec((tm, n), lambda i: (i, 0)) for n in out_widths],
        out_shape=[jax.ShapeDtypeStruct((m, n), dt) for n, dt in zip(out_widths, out_dtypes)],
        compiler_params=pltpu.CompilerParams(dimension_semantics=("parallel",), vmem_limit_bytes=VMEM_LIMIT),
        name="norm_in_proj",
    )(h2d, g.reshape(1, d).astype(F32), w)


def _moba_kernel(far_ref, q_ref, k_ref, v_ref, bias_ref, o_ref, kmean_ref, vt_ref, addm_ref, s_ref, *, nb):
    blk = MOBA_BLOCK
    hd = A_HEAD_DIM
    hp = pl.program_id(1)

    for j in range(nb):
        kb = k_ref[0, j * blk:(j + 1) * blk, :].astype(F32)
        kmean_ref[j:j + 1, :] = jnp.sum(kb, axis=0, keepdims=True) * (1.0 / blk)
        vt = v_ref[0, j * blk:(j + 1) * blk, :].astype(F32).T.astype(BF16)
        for hh in range(2):
            vt_ref[j, hh, 0:hd, :] = vt[hh * hd:(hh + 1) * hd, :]
            vt_ref[j, hh, hd:, :] = jnp.ones((ONES_ROWS, blk), BF16)
    addm_ref[:, nb:, :] = jnp.full((2, HALO, blk), NEG, F32)

    refs = (far_ref, q_ref, k_ref, bias_ref, o_ref, kmean_ref, vt_ref, addm_ref, s_ref)
    lax.fori_loop(0, nb, functools.partial(_moba_q_block, refs, hp, nb), 0)


def _moba_q_block(refs, hp, nb, qi, _):
    far_ref, q_ref, k_ref, bias_ref, o_ref, kmean_ref, vt_ref, addm_ref, s_ref = refs
    blk = MOBA_BLOCK
    hd = A_HEAD_DIM
    q_rows = pl.ds(pl.multiple_of(qi * blk, blk), blk)
    q = q_ref[0, q_rows, :]
    lane = lax.broadcasted_iota(I32, (1, 2 * hd), 1)
    rowj = lax.broadcasted_iota(I32, (nb, blk), 0)
    scale = hd ** -0.5
    qms = []
    for hh in range(2):
        in_head = (lane >= hh * hd) & (lane < (hh + 1) * hd)
        qm = jnp.where(in_head, q, jnp.zeros_like(q)) * scale
        qms.append(qm)
        km = jnp.where(in_head, kmean_ref[...], 0.0)
        gate = sum(_dot_nt(part, qm) for part in _split3(km))
        gate = jnp.where(rowj < qi, gate, -jnp.inf)
        ranks = []
        for j in range(nb):
            gj = gate[j:j + 1, :]
            ahead = jnp.where(gate > gj, 1.0, 0.0)
            if j > 0:
                ahead = ahead + jnp.where(rowj < j, jnp.where(gate == gj, 1.0, 0.0), 0.0)
            ranks.append(jnp.sum(ahead, axis=0, keepdims=True))
        rank = jnp.concatenate(ranks, axis=0)
        chosen_bias = jnp.where(rowj == qi - 1, 0.0, far_ref[hp * 2 + hh])
        addm_ref[hh, 0:nb, :] = jnp.where(rowj < qi, jnp.where(rank < float(MOBA_TOPK), chosen_bias, NEG), NEG)

    def k_rows(start, rows):
        return k_ref[0, pl.ds(pl.multiple_of(start, blk), rows), :]

    def item(i, slot, carry):
        t = jnp.minimum(i, nb // 2 - 1)
        j0, j1 = 2 * t, 2 * t + 1
        r1 = jnp.where(j1 <= qi - 2, j1, nb)
        kb = k_rows(j0 * blk, 2 * blk)
        ss = [_dot_nt(kb, qms[hh]) for hh in range(2)]
        blocks = (jnp.where(i == 0, pc, 2 * i - 2), jnp.where(i == 0, qi, 2 * i - 1))
        scores = [s_ref[slot, hh] for hh in range(2)]
        m_new = [jnp.maximum(carry[2 * hh], jnp.max(scores[hh], axis=0, keepdims=True)) for hh in range(2)]
        ps = [jnp.exp(scores[hh] - m_new[hh]).astype(BF16) for hh in range(2)]
        pv = [sum(_dot(vt_ref[j, hh], ps[hh][n * blk:(n + 1) * blk, :]) for n, j in enumerate(blocks))
              for hh in range(2)]
        out = ()
        for hh in range(2):
            alpha = jnp.exp(carry[2 * hh] - m_new[hh])
            out += (m_new[hh], alpha * carry[2 * hh + 1] + pv[hh])
        for hh in range(2):
            s_ref[1 - slot, hh, 0:blk, :] = ss[hh][:blk] + addm_ref[hh, pl.ds(j0, 1), :]
            s_ref[1 - slot, hh, blk:, :] = ss[hh][blk:] + addm_ref[hh, pl.ds(r1, 1), :]
        return out

    pc = jnp.maximum(qi - 1, 0)
    k_prev, k_own = k_rows(pc * blk, blk), k_rows(qi * blk, blk)
    for hh in range(2):
        s_ref[0, hh, 0:blk, :] = _dot_nt(k_prev, qms[hh]) + bias_ref[hh, 1] + addm_ref[hh, pl.ds(pc, 1), :]
        s_ref[0, hh, blk:, :] = _dot_nt(k_own, qms[hh]) + bias_ref[hh, 0]
    n_items = jnp.maximum(qi, 1) // 2 + 1

    start = (jnp.full((1, blk), -jnp.inf, F32), jnp.zeros((hd + ONES_ROWS, blk), F32))
    carry = lax.fori_loop(0, n_items // 2, lambda t, c: item(2 * t + 1, 1, item(2 * t, 0, c)), start + start)
    carry = lax.cond(lax.rem(n_items, 2) == 1, lambda c: item(n_items - 1, 0, c), lambda c: c, carry)
    outs = [carry[2 * hh + 1][:hd] / carry[2 * hh + 1][hd:hd + 1] for hh in range(2)]
    o_ref[0, q_rows, :] = jnp.concatenate(outs, axis=0).T.astype(o_ref.dtype)
    return 0


def _moba(zqkv, bias_tiles, far_bias):
    bsz, s, _ = zqkv.shape
    nb = s // MOBA_BLOCK
    n_pairs = A_HEADS // 2
    pair_w = 2 * A_HEAD_DIM
    return pl.pallas_call(
        functools.partial(_moba_kernel, nb=nb),
        grid=(bsz, n_pairs),
        in_specs=[pl.BlockSpec(memory_space=pltpu.SMEM),
                  pl.BlockSpec((1, s, pair_w), lambda b, hp: (b, 0, hp)),
                  pl.BlockSpec((1, s, pair_w), lambda b, hp: (b, 0, n_pairs + hp)),
                  pl.BlockSpec((1, s, pair_w), lambda b, hp: (b, 0, 2 * n_pairs + hp)),
                  pl.BlockSpec((2, 2, MOBA_BLOCK, MOBA_BLOCK), lambda b, hp: (hp, 0, 0, 0))],
        out_specs=pl.BlockSpec((1, s, pair_w), lambda b, hp: (b, 0, hp)),
        out_shape=jax.ShapeDtypeStruct((bsz, s, A_WIDTH), BF16),
        scratch_shapes=[pltpu.VMEM((nb, pair_w), F32),
                        pltpu.VMEM((nb, 2, A_HEAD_DIM + ONES_ROWS, MOBA_BLOCK), BF16),
                        pltpu.VMEM((2, nb + HALO, MOBA_BLOCK), F32),
                        pltpu.VMEM((2, 2, 2 * MOBA_BLOCK, MOBA_BLOCK), F32)],
        compiler_params=pltpu.CompilerParams(dimension_semantics=("parallel", "parallel"),
                                             vmem_limit_bytes=VMEM_LIMIT),
        name="moba_attention",
    )(far_bias, zqkv, zqkv, zqkv, bias_tiles)


def _mlstm_kernel(zb_ref, zg_ref, cw_ref, cb_ref, gb_ref, ng_ref, o_ref, hist_ref, state_ref, m_ref, x_ref):
    @pl.when(pl.program_id(1) == 0)
    def _():
        hist_ref[...] = jnp.zeros_like(hist_ref)
        state_ref[...] = jnp.zeros_like(state_ref)
        m_ref[...] = jnp.zeros_like(m_ref)

    refs = (zb_ref, zg_ref, cw_ref, cb_ref, gb_ref, ng_ref, o_ref, hist_ref, state_ref, m_ref, x_ref)
    lax.fori_loop(0, zb_ref.shape[1] // CHUNK, functools.partial(_mlstm_chunk, refs), 0)


def _mlstm_chunk(refs, j, _):
    zb_ref, zg_ref, cw_ref, cb_ref, gb_ref, ng_ref, o_ref, hist_ref, state_ref, m_ref, x_ref = refs
    L = CHUNK
    dh = B_HEAD_DIM
    rows = pl.ds(pl.multiple_of(j * L, L), L)

    x = zb_ref[0, rows, 0:2 * B_WIDTH].astype(F32)
    lanes = x_ref.shape[2]
    ys = []
    for k in range(x_ref.shape[0]):
        ls = slice(k * lanes, (k + 1) * lanes)
        x_ref[k, 0:HALO, :] = hist_ref[:, ls]
        x_ref[k, HALO:, :] = x[:, ls]
        yk = cw_ref[MLSTM_CONV - 1:MLSTM_CONV, ls] * x[:, ls]
        for back in range(1, MLSTM_CONV):
            yk = yk + cw_ref[MLSTM_CONV - 1 - back:MLSTM_CONV - back, ls] * x_ref[k, HALO - back:HALO - back + L, :]
        ys.append(yk)
    hist_ref[...] = x[L - HALO:, :]
    y = jnp.concatenate(ys, axis=1) + cb_ref[...]
    y = y * jax.nn.sigmoid(y)
    q_all = y[:, :B_WIDTH]
    k_all = y[:, B_WIDTH:] * (dh ** -0.5)

    g = zg_ref[0, rows, :] + gb_ref[...]
    lane = lax.broadcasted_iota(I32, (1, 128), 1)
    log_f = jnp.minimum(g, 0.0) - jnp.log(1.0 + jnp.exp(-jnp.abs(g)))
    gates = jnp.where(lane < B_HEADS, g, log_f)
    row = lax.broadcasted_iota(I32, (L, L), 0)
    col = lax.broadcasted_iota(I32, (L, L), 1)
    causal = row >= col
    tril = jnp.where(causal, 1.0, 0.0).astype(BF16)
    triu = jnp.where(row <= col, 1.0, 0.0).astype(BF16)
    csum_col = sum(_dot(tril, part) for part in _split3(gates))
    gates_t = gates.T
    csum_row = sum(_dot(part, triu) for part in _split3(gates_t[0:16, :]))
    top_row = lax.broadcasted_iota(I32, (dh, L), 0) == 0

    heads = range(B_HEADS)
    sls = [slice(h * dh, (h + 1) * dh) for h in heads]
    qhs = [q_all[:, sls[h]].astype(BF16) for h in heads]
    khs = [k_all[:, sls[h]].astype(BF16) for h in heads]
    states = [state_ref[h] for h in heads]
    m_prevs = [m_ref[h] for h in heads]
    qk_t = [_dot_nt(khs[h], qhs[h]) for h in heads]
    inter_t = [_dot_nt(states[h].astype(BF16), qhs[h]) for h in heads]
    v_t = [zb_ref[0, rows, 2 * B_WIDTH + h * dh:2 * B_WIDTH + (h + 1) * dh].astype(F32).T for h in heads]
    ones_t = jnp.where(top_row, 1.0, 0.0).astype(BF16)
    vaug_t = [jnp.concatenate([v_t[h].astype(BF16), ones_t], axis=0) for h in heads]

    c_cols = [gates[:, h:h + 1] - csum_col[:, B_HEADS + h:B_HEADS + h + 1] for h in heads]
    b_rows = [csum_row[B_HEADS + h:B_HEADS + h + 1, :] for h in heads]
    i_rows = [gates_t[h:h + 1, :] for h in heads]
    log_inter = [b_rows[h] + m_prevs[h] for h in heads]
    log_intra = [jnp.where(row <= col, b_rows[h] + c_cols[h], -jnp.inf) for h in heads]
    m_t = [jnp.maximum(log_inter[h], jnp.max(log_intra[h], axis=0, keepdims=True)) for h in heads]
    decay = [jnp.exp(log_intra[h] - m_t[h]) for h in heads]
    w_inter = [jnp.exp(log_inter[h] - m_t[h]) for h in heads]

    nd = [w_inter[h] * inter_t[h] + _dot(vaug_t[h], (qk_t[h] * decay[h]).astype(BF16)) for h in heads]

    b_last = [b_rows[h][:, L - 1:L] for h in heads]
    log_s = [b_last[h] - b_rows[h] + i_rows[h] for h in heads]
    m_new = [jnp.maximum(b_last[h] + m_prevs[h], jnp.max(log_s[h], axis=1, keepdims=True)) for h in heads]
    w_prev = [jnp.exp(b_last[h] + m_prevs[h] - m_new[h]) for h in heads]
    w_s = [jnp.exp(log_s[h] - m_new[h]) for h in heads]
    for h in heads:
        weighted = jnp.concatenate([v_t[h] * w_s[h], jnp.where(top_row, w_s[h], 0.0)], axis=0).astype(BF16)
        state_ref[h] = w_prev[h] * states[h] + _dot(weighted, khs[h])
        m_ref[h] = m_new[h]

    for h in heads:
        den = nd[h][dh:dh + 1, :]
        hout = nd[h][:dh, :] / jnp.maximum(jnp.abs(den), jnp.exp(-m_t[h]))
        hn = hout * lax.rsqrt(jnp.mean(hout * hout, axis=0, keepdims=True) + EPS) * ng_ref[sls[h], :]
        og = jax.nn.sigmoid(zb_ref[0, rows, 3 * B_WIDTH + h * dh:3 * B_WIDTH + (h + 1) * dh].astype(F32))
        o_ref[0, rows, sls[h]] = (hn.T * og).astype(o_ref.dtype)
    return 0


def _mlstm(zb, zg, conv_w, conv_b, b_i, b_f, norm_g):
    bsz, s, _ = zb.shape
    rows = min(MIXER_ROWS, s)
    gate_bias = jnp.zeros((1, 128), F32).at[0, :B_HEADS].set(b_i.astype(F32)).at[0, B_HEADS:2 * B_HEADS].set(
        b_f.astype(F32))
    return pl.pallas_call(
        _mlstm_kernel,
        grid=(bsz, s // rows),
        in_specs=[pl.BlockSpec((1, rows, 4 * B_WIDTH), lambda b, c: (b, c, 0)),
                  pl.BlockSpec((1, rows, 128), lambda b, c: (b, c, 0)),
                  _resident((MLSTM_CONV, 2 * B_WIDTH)), _resident((1, 2 * B_WIDTH)),
                  _resident((1, 128)), _resident((B_WIDTH, 1))],
        out_specs=pl.BlockSpec((1, rows, B_WIDTH), lambda b, c: (b, c, 0)),
        out_shape=jax.ShapeDtypeStruct((bsz, s, B_WIDTH), BF16),
        scratch_shapes=[pltpu.VMEM((HALO, 2 * B_WIDTH), F32),
                        pltpu.VMEM((B_HEADS, 2 * B_HEAD_DIM, B_HEAD_DIM), F32),
                        pltpu.VMEM((B_HEADS, 1, 1), F32),
                        pltpu.VMEM((2 * B_WIDTH // 128, HALO + CHUNK, 128), F32)],
        compiler_params=pltpu.CompilerParams(dimension_semantics=("parallel", "arbitrary"),
                                             vmem_limit_bytes=VMEM_LIMIT),
        name="mlstm",
    )(zb, zg.reshape(bsz, s, 128), conv_w.astype(F32), conv_b.reshape(1, -1).astype(F32), gate_bias,
      norm_g.reshape(-1, 1).astype(F32))


def _pool_ret_kernel(z_ref, cos_ref, sin_ref, swap_ref, dq_ref, dk_ref, dc_ref, dm_ref, bd_ref, pw_ref, ps_ref,
                     rg_ref, o_ref, hist_ref, state_ref, win_ref):
    @pl.when(pl.program_id(1) == 0)
    def _():
        hist_ref[...] = jnp.zeros_like(hist_ref)
        state_ref[...] = jnp.zeros_like(state_ref)

    refs = (z_ref, cos_ref, sin_ref, swap_ref, dq_ref, dk_ref, dc_ref, dm_ref, bd_ref, pw_ref, ps_ref, rg_ref, o_ref,
            hist_ref, state_ref, win_ref)
    lax.fori_loop(0, z_ref.shape[1] // CHUNK, functools.partial(_pool_ret_chunk, refs), 0)


def _pool_ret_chunk(refs, j, _):
    (z_ref, cos_ref, sin_ref, swap_ref, dq_ref, dk_ref, dc_ref, dm_ref, bd_ref, pw_ref, ps_ref, rg_ref, o_ref,
     hist_ref, state_ref, win_ref) = refs
    L = CHUNK
    rows = pl.ds(pl.multiple_of(j * L, L), L)
    c = pl.program_id(1) * (z_ref.shape[1] // L) + j

    u = z_ref[0, rows, 0:POOL_WIDTH].astype(F32)
    pos1 = (c * L + lax.broadcasted_iota(I32, (L, 1), 0) + 1).astype(F32)
    gd = POOL_GROUP_DIM
    for g, w in enumerate(POOL_WINDOWS):
        sl = slice(g * gd, (g + 1) * gd)
        w_ref = win_ref.at[g]
        w_ref[0:POOL_HALO, :] = hist_ref[:, sl]
        w_ref[POOL_HALO:, :] = u[:, sl]
        span = 1
        while span < w:
            n_sum = POOL_HALO + L - span
            w_ref[span:, :] = w_ref[span:span + n_sum, :] + w_ref[0:n_sum, :]
            span *= 2
        pooled = w_ref[POOL_HALO:, :] / jnp.minimum(pos1, float(w)) - u[:, sl]
        o_ref[0, rows, sl] = (_dot(pooled.astype(BF16), pw_ref[g]) * ps_ref[:, sl]).astype(o_ref.dtype)

    rq = z_ref[0, rows, POOL_WIDTH:POOL_WIDTH + R_QK_WIDTH]
    rk = z_ref[0, rows, POOL_WIDTH + R_QK_WIDTH:POOL_WIDTH + 2 * R_QK_WIDTH]
    v0 = POOL_WIDTH + 2 * R_QK_WIDTH
    rv = z_ref[0, rows, v0:v0 + R_V_WIDTH]
    cosv, sinv = cos_ref[rows, :], sin_ref[rows, :]
    q_rot = rq.astype(F32) * cosv + _dot(rq, swap_ref[...]) * sinv
    k_rot = (rk.astype(F32) * cosv + _dot(rk, swap_ref[...]) * sinv) * (R_QK_DIM ** -0.5)
    qb = q_rot.astype(BF16)
    kb = k_rot.astype(BF16)
    inter = _dot((q_rot * dq_ref[...]).astype(BF16), state_ref[...].astype(BF16))
    lane = lax.broadcasted_iota(I32, (1, R_QK_WIDTH), 1)
    g0 = v0 + R_V_WIDTH
    for h in range(R_HEADS):
        in_head = (lane >= h * R_QK_DIM) & (lane < (h + 1) * R_QK_DIM)
        sc = _dot_nt(jnp.where(in_head, qb, jnp.zeros_like(qb)), kb) * dm_ref[h]
        vs = slice(h * R_V_DIM, (h + 1) * R_V_DIM)
        y = _dot(sc.astype(BF16), rv[:, vs]) + inter[:, vs]
        gate = z_ref[0, rows, g0 + h * R_V_DIM:g0 + (h + 1) * R_V_DIM].astype(F32)
        gate = gate * jax.nn.sigmoid(gate)
        o_ref[0, rows, POOL_WIDTH + h * R_V_DIM:POOL_WIDTH + (h + 1) * R_V_DIM] = (
            _rms(y, rg_ref[:, vs]) * gate).astype(o_ref.dtype)
    hist_ref[...] = u[L - POOL_HALO:, :]
    upd = _dot((k_rot * dk_ref[...]).T.astype(BF16), rv)
    state_ref[...] = dc_ref[...] * state_ref[...] + upd * bd_ref[...]
    return 0


def _retention_tables(s):
    L = CHUNK
    half = R_QK_DIM // 2
    inv = ROPE_BASE ** (-jnp.arange(half, dtype=F32) / half)
    ang = jnp.arange(s, dtype=F32)[:, None] * inv[None, :]
    cos, sin = jnp.cos(ang), jnp.sin(ang)
    cos_t = jnp.tile(jnp.concatenate([cos, cos], axis=-1), (1, R_HEADS))
    sin_t = jnp.tile(jnp.concatenate([-sin, sin], axis=-1), (1, R_HEADS))
    j = np.arange(R_QK_WIDTH)
    src = (j // R_QK_DIM) * R_QK_DIM + (j % R_QK_DIM + half) % R_QK_DIM
    swap = np.zeros((R_QK_WIDTH, R_QK_WIDTH), np.float32)
    swap[src, j] = 1.0
    log_g = jnp.log(1.0 - jnp.exp2(-5.0 - jnp.arange(R_HEADS, dtype=F32)))
    t = jnp.arange(L, dtype=F32)
    rel = t[:, None] - t[None, :]
    decay_intra = jnp.where(rel >= 0, jnp.exp(jnp.maximum(rel, 0.0) * log_g[:, None, None]), 0.0)
    decay_q = jnp.exp((t + 1.0) * log_g[:, None])
    decay_k = jnp.exp((L - 1.0 - t) * log_g[:, None])
    decay_chunk = jnp.exp(L * log_g)
    dq = jnp.repeat(decay_q.T, R_QK_DIM, axis=1)
    dk = jnp.repeat(decay_k.T, R_QK_DIM, axis=1)
    dc = jnp.repeat(decay_chunk, R_QK_DIM)[:, None]
    bd = (np.arange(R_QK_WIDTH)[:, None] // R_QK_DIM == np.arange(R_V_WIDTH)[None, :] // R_V_DIM)
    return cos_t, sin_t, jnp.asarray(swap, BF16), dq, dk, dc, decay_intra, jnp.asarray(bd, F32)


def _pool_ret(z, pool_w, pool_scale, ret_g):
    bsz, s, width = z.shape
    cos_t, sin_t, swap, dq, dk, dc, dm, bd = _retention_tables(s)
    rows = min(MIXER_ROWS, s)
    return pl.pallas_call(
        _pool_ret_kernel,
        grid=(bsz, s // rows),
        in_specs=[pl.BlockSpec((1, rows, width), lambda b, c: (b, c, 0)),
                  pl.BlockSpec((rows, R_QK_WIDTH), lambda b, c: (c, 0)),
                  pl.BlockSpec((rows, R_QK_WIDTH), lambda b, c: (c, 0)),
                  _resident(swap.shape), _resident(dq.shape), _resident(dk.shape), _resident(dc.shape),
                  _resident(dm.shape), _resident(bd.shape), _resident(pool_w.shape),
                  _resident((1, POOL_WIDTH)), _resident((1, R_V_WIDTH))],
        out_specs=pl.BlockSpec((1, rows, POOL_WIDTH + R_V_WIDTH), lambda b, c: (b, c, 0)),
        out_shape=jax.ShapeDtypeStruct((bsz, s, POOL_WIDTH + R_V_WIDTH), BF16),
        scratch_shapes=[pltpu.VMEM((POOL_HALO, POOL_WIDTH), F32),
                        pltpu.VMEM((R_QK_WIDTH, R_V_WIDTH), F32),
                        pltpu.VMEM((len(POOL_WINDOWS), POOL_HALO + CHUNK, POOL_GROUP_DIM), F32)],
        compiler_params=pltpu.CompilerParams(dimension_semantics=("parallel", "arbitrary"),
                                             vmem_limit_bytes=VMEM_LIMIT),
        name="pool_retention",
    )(z, cos_t, sin_t, swap, dq, dk, dc, dm, bd, pool_w.astype(BF16), pool_scale.reshape(1, -1).astype(F32),
      ret_g.reshape(1, -1).astype(F32))


def _ffn_kernel(h_ref, a1_ref, a2_ref, wo_ref, g_ref, wup_ref, cw_ref, cb_ref, wd_ref, gout_ref, o_ref,
                hist_ref, y_ref, x_ref, a_ref, *, n_tiles, final_norm):
    tm = h_ref.shape[1]
    lanes = x_ref.shape[2]
    n_lane_tiles = FF_TILE // lanes

    @pl.when(pl.program_id(1) == 0)
    def _():
        hist_ref[...] = jnp.zeros_like(hist_ref)

    half = a1_ref.shape[2]
    h1 = h_ref[0] + _dot(a1_ref[0], wo_ref[0:half, :]) + _dot(a2_ref[0], wo_ref[half:, :])
    y_ref[...] = _rms(h1, g_ref[...]).astype(BF16)
    o_ref[0] = h1

    def cols(idx):
        return pl.ds(pl.multiple_of(idx * FF_TILE, FF_TILE), FF_TILE)

    def conv(u, idx, slot):
        w = cw_ref[:, cols(idx)]
        outs = []
        for k in range(n_lane_tiles):
            ls = slice(k * lanes, (k + 1) * lanes)
            xr = x_ref.at[slot * n_lane_tiles + k]
            xr[0:HALO, :] = hist_ref[idx, :, ls]
            xr[HALO:, :] = u[:, ls]
            hist_ref[idx, :, ls] = xr[tm:tm + HALO, :]
            outs.append(w[2:3, ls] * xr[HALO:HALO + tm, :] + w[1:2, ls] * xr[HALO - 1:HALO - 1 + tm, :]
                        + w[0:1, ls] * xr[HALO - 2:HALO - 2 + tm, :])
        return jnp.concatenate(outs, axis=1) + cb_ref[:, cols(idx)]

    def tile(c, _):
        y = y_ref[...]
        slot = 2 * lax.rem(c, 2)
        gate = conv(_dot(y, wup_ref[:, cols(c)]), c, slot)
        val = conv(_dot(y, wup_ref[:, cols(n_tiles + c)]), n_tiles + c, slot + 1)
        a_ref[:, cols(c)] = (gate * jax.nn.sigmoid(gate) * val).astype(BF16)
        return 0

    lax.fori_loop(0, n_tiles, tile, 0, unroll=2)
    o_ref[0] += _dot(a_ref[...], wd_ref[...])
    if final_norm:
        o_ref[0] = _rms(o_ref[0], gout_ref[...])


def _ffn(h, a1, a2, col1, col2, w_out, norm_g, w_up, conv_w, conv_b, w_down, out_g, final_norm):
    bsz, s, d = h.shape
    tm = min(FFN_ROWS, s)
    half = w_out.shape[0] // 2
    n_tiles = D_FF // FF_TILE
    wo = w_out.astype(BF16)
    wup = w_up.astype(BF16)
    cw = conv_w.astype(F32)
    cb = conv_b.astype(F32).reshape(1, -1)
    wd = w_down.astype(BF16)
    return pl.pallas_call(
        functools.partial(_ffn_kernel, n_tiles=n_tiles, final_norm=final_norm),
        grid=(bsz, s // tm),
        in_specs=[pl.BlockSpec((1, tm, d), lambda b, t: (b, t, 0)),
                  pl.BlockSpec((1, tm, half), lambda b, t: (b, t, col1)),
                  pl.BlockSpec((1, tm, half), lambda b, t: (b, t, col2)),
                  _resident((2 * half, d)), _resident((1, d)),
                  _resident(wup.shape), _resident(cw.shape), _resident(cb.shape), _resident(wd.shape),
                  _resident((1, d))],
        out_specs=pl.BlockSpec((1, tm, d), lambda b, t: (b, t, 0)),
        out_shape=jax.ShapeDtypeStruct((bsz, s, d), F32),
        scratch_shapes=[pltpu.VMEM((2 * n_tiles, HALO, FF_TILE), F32),
                        pltpu.VMEM((tm, d), BF16),
                        pltpu.VMEM((4 * FF_TILE // 128, HALO + tm, 128), F32),
                        pltpu.VMEM((tm, D_FF), BF16)],
        compiler_params=pltpu.CompilerParams(dimension_semantics=("parallel", "arbitrary"),
                                             vmem_limit_bytes=VMEM_LIMIT),
        name="out_proj_ffn",
    )(h, a1, a2, wo, norm_g.reshape(1, d).astype(F32), wup, cw, cb, wd, out_g.reshape(1, d).astype(F32))


def kernel(x, rel_bias, norm_mix_g, norm_ffn_g, norm_out_g, w_in_ab, w_out_ab, mlstm_conv_w, mlstm_conv_b, mlstm_b_i,
           mlstm_b_f, mlstm_norm_g, w_in_cd, w_out_cd, pool_w, pool_scale, ret_norm_g, ffn_w_up, ffn_conv_w,
           ffn_conv_b, ffn_w_down):
    bsz, s, d = x.shape
    m = bsz * s

    n_attn = 3 * A_WIDTH
    n_main = n_attn + 4 * B_WIDTH
    n_gate = 128
    w0 = jnp.pad(w_in_ab[0].astype(BF16), ((0, 0), (0, n_main + n_gate - w_in_ab.shape[2])))
    zqkv, zb, zg = _norm_proj(x.reshape(m, d), norm_mix_g[0], w0, [n_attn, n_main - n_attn, n_gate],
                              [BF16, BF16, F32])
    bias_tiles = _bias_tables(rel_bias)
    far_bias = rel_bias[REL_BUCKETS - 1].astype(F32)
    ya = _moba(zqkv.reshape(bsz, s, n_attn), bias_tiles, far_bias)
    yb = _mlstm(zb.reshape(bsz, s, 4 * B_WIDTH), zg, mlstm_conv_w[0], mlstm_conv_b[0], mlstm_b_i[0], mlstm_b_f[0],
                mlstm_norm_g[0])
    h = _ffn(x, ya, yb, 0, 0, w_out_ab[0], norm_ffn_g[0], ffn_w_up[0], ffn_conv_w[0], ffn_conv_b[0], ffn_w_down[0],
             norm_out_g, False)

    (z1,) = _norm_proj(h.reshape(m, d), norm_mix_g[1], w_in_cd[0].astype(BF16), [w_in_cd.shape[2]], [BF16])
    cat = _pool_ret(z1.reshape(bsz, s, -1), pool_w[0], pool_scale[0], ret_norm_g[0])
    return _ffn(h, cat, cat, 0, 1, w_out_cd[0], norm_ffn_g[1], ffn_w_up[1], ffn_conv_w[1], ffn_conv_b[1],
                ffn_w_down[1], norm_out_g, True)
```

```python
import functools
import math

import numpy as np
import jax
import jax.numpy as jnp
from jax import lax
from jax.experimental import pallas as pl
from jax.experimental.pallas import tpu as pltpu

F32 = jnp.float32
BF16 = jnp.bfloat16
I32 = jnp.int32

EPS = 1e-6
NEG = -1e30

D_MODEL = 1024
A_HEADS = 8
A_HEAD_DIM = 64
A_WIDTH = 512
MOBA_BLOCK = 256
MOBA_TOPK = 3
REL_BUCKETS = 32
REL_MAX_DIST = 128
B_HEADS = 4
B_HEAD_DIM = 128
B_WIDTH = 512
MLSTM_CONV = 4
POOL_WINDOWS = (2, 4, 8, 16)
POOL_GROUP_DIM = 128
POOL_WIDTH = 512
R_HEADS = 4
R_QK_DIM = 64
R_V_DIM = 128
R_QK_WIDTH = 256
R_V_WIDTH = 512
ROPE_BASE = 10000.0
D_FF = 2816
FFN_CONV = 3

CHUNK = 256
ONES_ROWS = 16
HALO = 8
POOL_HALO = 16
FF_TILE = 256
PROJ_ROWS = 1024
FFN_ROWS = 1024
MIXER_ROWS = 1024
VMEM_LIMIT = 60 * 1024 * 1024

_NT = (((1,), (1,)), ((), ()))


def _dot(a, b):
    return jnp.dot(a, b, preferred_element_type=F32)


def _dot_nt(a, b):
    return lax.dot_general(a, b, _NT, preferred_element_type=F32)


def _split3(a):
    a1 = a.astype(BF16)
    r = a - a1.astype(F32)
    a2 = r.astype(BF16)
    a3 = (r - a2.astype(F32)).astype(BF16)
    return a1, a2, a3


def _rms(x, g):
    return x * lax.rsqrt(jnp.mean(x * x, axis=-1, keepdims=True) + EPS) * g


def _resident(shape):
    n = len(shape)
    return pl.BlockSpec(shape, lambda *_: (0,) * n, pipeline_mode=pl.Buffered(1))


def _resident_layer(stacked, layer):
    n = stacked.ndim - 1
    return pl.BlockSpec((None,) + stacked.shape[1:], lambda *_: (layer,) + (0,) * n, pipeline_mode=pl.Buffered(1))


def _bucket_thresholds():
    max_exact = REL_BUCKETS // 2
    d = np.arange(1, 4 * REL_MAX_DIST, dtype=np.float32)
    large = max_exact + (np.log(d / np.float32(max_exact)) / np.float32(math.log(REL_MAX_DIST / max_exact))
                         * np.float32(REL_BUCKETS - max_exact)).astype(np.int32)
    large = np.minimum(large, REL_BUCKETS - 1)
    return [int(d[np.argmax(large >= b)]) for b in range(max_exact + 1, REL_BUCKETS)]


def _bias_kernel(tab_ref, o_ref, *, thresholds):
    h = pl.program_id(0)
    row = lax.broadcasted_iota(I32, (MOBA_BLOCK, MOBA_BLOCK), 0)
    col = lax.broadcasted_iota(I32, (MOBA_BLOCK, MOBA_BLOCK), 1)
    max_exact = REL_BUCKETS // 2
    for t, off in enumerate((0, MOBA_BLOCK)):
        dist = col - row + off
        d = jnp.maximum(dist, 0)
        large = jnp.full(d.shape, max_exact, I32)
        for thr in thresholds:
            large = large + jnp.where(d >= thr, 1, 0)
        bucket = jnp.where(d < max_exact, d, large)
        val = jnp.zeros(d.shape, F32)
        for i in range(REL_BUCKETS):
            val = jnp.where(bucket == i, tab_ref[h, i], val)
        if off == 0:
            val = jnp.where(dist >= 0, val, NEG)
        o_ref[0, t] = val


def _bias_tables(rel_bias):
    tab = rel_bias.astype(F32).T
    return pl.pallas_call(
        functools.partial(_bias_kernel, thresholds=_bucket_thresholds()),
        grid=(A_HEADS,),
        in_specs=[pl.BlockSpec(memory_space=pltpu.SMEM)],
        out_specs=pl.BlockSpec((1, 2, MOBA_BLOCK, MOBA_BLOCK), lambda h: (h, 0, 0, 0)),
        out_shape=jax.ShapeDtypeStruct((A_HEADS, 2, MOBA_BLOCK, MOBA_BLOCK), F32),
        name="rel_bias_tiles",
    )(tab)


def _norm_proj_kernel(h_ref, g_ref, w_ref, *o_refs, col_tile):
    y = _rms(h_ref[...], g_ref[...]).astype(BF16)
    base = 0
    for o_ref in o_refs:
        n = o_ref.shape[1]
        for c0 in range(0, n, col_tile):
            c1 = min(c0 + col_tile, n)
            o_ref[:, c0:c1] = _dot(y, w_ref[:, base + c0:base + c1]).astype(o_ref.dtype)
        base += n


def _norm_proj(h2d, g, w, out_widths, out_dtypes):
    m, d = h2d.shape
    tm = min(PROJ_ROWS, m)
    assert sum(out_widths) == w.shape[1]
    return pl.pallas_call(
        functools.partial(_norm_proj_kernel, col_tile=512),
        grid=(m // tm,),
        in_specs=[pl.BlockSpec((tm, d), lambda i: (i, 0)), _resident((1, d)), _resident(w.shape)],
        out_specs=[pl.BlockSpec((tm, n), lambda i: (i, 0)) for n in out_widths],
        out_shape=[jax.ShapeDtypeStruct((m, n), dt) for n, dt in zip(out_widths, out_dtypes)],
        compiler_params=pltpu.CompilerParams(dimension_semantics=("parallel",), vmem_limit_bytes=VMEM_LIMIT),
        name="norm_in_proj",
    )(h2d, g.reshape(1, d).astype(F32), w)


def _moba_kernel(far_ref, q_ref, k_ref, v_ref, bias_ref, o_ref, kmean_ref, vt_ref, addm_ref, s_ref, *, nb):
    blk = MOBA_BLOCK
    hd = A_HEAD_DIM
    hp = pl.program_id(1)

    for j in range(nb):
        kb = k_ref[0, j * blk:(j + 1) * blk, :].astype(F32)
        kmean_ref[j:j + 1, :] = jnp.sum(kb, axis=0, keepdims=True) * (1.0 / blk)
        vt = v_ref[0, j * blk:(j + 1) * blk, :].astype(F32).T.astype(BF16)
        for hh in range(2):
            vt_ref[j, hh, 0:hd, :] = vt[hh * hd:(hh + 1) * hd, :]
            vt_ref[j, hh, hd:, :] = jnp.ones((ONES_ROWS, blk), BF16)
    addm_ref[:, nb:, :] = jnp.full((2, HALO, blk), NEG, F32)

    refs = (far_ref, q_ref, k_ref, bias_ref, o_ref, kmean_ref, vt_ref, addm_ref, s_ref)
    lax.fori_loop(0, nb, functools.partial(_moba_q_block, refs, hp, nb), 0)


def _moba_q_block(refs, hp, nb, qi, _):
    far_ref, q_ref, k_ref, bias_ref, o_ref, kmean_ref, vt_ref, addm_ref, s_ref = refs
    blk = MOBA_BLOCK
    hd = A_HEAD_DIM
    q_rows = pl.ds(pl.multiple_of(qi * blk, blk), blk)
    q = q_ref[0, q_rows, :]
    lane = lax.broadcasted_iota(I32, (1, 2 * hd), 1)
    rowj = lax.broadcasted_iota(I32, (nb, blk), 0)
    scale = hd ** -0.5
    qms = []
    for hh in range(2):
        in_head = (lane >= hh * hd) & (lane < (hh + 1) * hd)
        qm = jnp.where(in_head, q, jnp.zeros_like(q)) * scale
        qms.append(qm)
        km = jnp.where(in_head, kmean_ref[...], 0.0)
        gate = sum(_dot_nt(part, qm) for part in _split3(km))
        gate = jnp.where(rowj < qi, gate, -jnp.inf)
        ranks = []
        for j in range(nb):
            gj = gate[j:j + 1, :]
            ahead = jnp.where(gate > gj, 1.0, 0.0)
            if j > 0:
                ahead = ahead + jnp.where(rowj < j, jnp.where(gate == gj, 1.0, 0.0), 0.0)
            ranks.append(jnp.sum(ahead, axis=0, keepdims=True))
        rank = jnp.concatenate(ranks, axis=0)
        chosen_bias = jnp.where(rowj == qi - 1, 0.0, far_ref[hp * 2 + hh])
        addm_ref[hh, 0:nb, :] = jnp.where(rowj < qi, jnp.where(rank < float(MOBA_TOPK), chosen_bias, NEG), NEG)

    def k_rows(start, rows):
        return k_ref[0, pl.ds(pl.multiple_of(start, blk), rows), :]

    def item(i, slot, carry):
        t = jnp.minimum(i, nb // 2 - 1)
        j0, j1 = 2 * t, 2 * t + 1
        r1 = jnp.where(j1 <= qi - 2, j1, nb)
        kb = k_rows(j0 * blk, 2 * blk)
        ss = [_dot_nt(kb, qms[hh]) for hh in range(2)]
        blocks = (jnp.where(i == 0, pc, 2 * i - 2), jnp.where(i == 0, qi, 2 * i - 1))
        scores = [s_ref[slot, hh] for hh in range(2)]
        m_new = [jnp.maximum(carry[2 * hh], jnp.max(scores[hh], axis=0, keepdims=True)) for hh in range(2)]
        ps = [jnp.exp(scores[hh] - m_new[hh]).astype(BF16) for hh in range(2)]
        pv = [sum(_dot(vt_ref[j, hh], ps[hh][n * blk:(n + 1) * blk, :]) for n, j in enumerate(blocks))
              for hh in range(2)]
        out = ()
        for hh in range(2):
            alpha = jnp.exp(carry[2 * hh] - m_new[hh])
            out += (m_new[hh], alpha * carry[2 * hh + 1] + pv[hh])
        for hh in range(2):
            s_ref[1 - slot, hh, 0:blk, :] = ss[hh][:blk] + addm_ref[hh, pl.ds(j0, 1), :]
            s_ref[1 - slot, hh, blk:, :] = ss[hh][blk:] + addm_ref[hh, pl.ds(r1, 1), :]
        return out

    pc = jnp.maximum(qi - 1, 0)
    k_prev, k_own = k_rows(pc * blk, blk), k_rows(qi * blk, blk)
    for hh in range(2):
        s_ref[0, hh, 0:blk, :] = _dot_nt(k_prev, qms[hh]) + bias_ref[hh, 1] + addm_ref[hh, pl.ds(pc, 1), :]
        s_ref[0, hh, blk:, :] = _dot_nt(k_own, qms[hh]) + bias_ref[hh, 0]
    n_items = jnp.maximum(qi, 1) // 2 + 1

    start = (jnp.full((1, blk), -jnp.inf, F32), jnp.zeros((hd + ONES_ROWS, blk), F32))
    carry = lax.fori_loop(0, n_items // 2, lambda t, c: item(2 * t + 1, 1, item(2 * t, 0, c)), start + start)
    carry = lax.cond(lax.rem(n_items, 2) == 1, lambda c: item(n_items - 1, 0, c), lambda c: c, carry)
    outs = [carry[2 * hh + 1][:hd] / carry[2 * hh + 1][hd:hd + 1] for hh in range(2)]
    o_ref[0, q_rows, :] = jnp.concatenate(outs, axis=0).T.astype(o_ref.dtype)
    return 0


def _moba(zqkv, bias_tiles, far_bias):
    bsz, s, _ = zqkv.shape
    nb = s // MOBA_BLOCK
    n_pairs = A_HEADS // 2
    pair_w = 2 * A_HEAD_DIM
    return pl.pallas_call(
        functools.partial(_moba_kernel, nb=nb),
        grid=(bsz, n_pairs),
        in_specs=[pl.BlockSpec(memory_space=pltpu.SMEM),
                  pl.BlockSpec((1, s, pair_w), lambda b, hp: (b, 0, hp)),
                  pl.BlockSpec((1, s, pair_w), lambda b, hp: (b, 0, n_pairs + hp)),
                  pl.BlockSpec((1, s, pair_w), lambda b, hp: (b, 0, 2 * n_pairs + hp)),
                  pl.BlockSpec((2, 2, MOBA_BLOCK, MOBA_BLOCK), lambda b, hp: (hp, 0, 0, 0))],
        out_specs=pl.BlockSpec((1, s, pair_w), lambda b, hp: (b, 0, hp)),
        out_shape=jax.ShapeDtypeStruct((bsz, s, A_WIDTH), BF16),
        scratch_shapes=[pltpu.VMEM((nb, pair_w), F32),
                        pltpu.VMEM((nb, 2, A_HEAD_DIM + ONES_ROWS, MOBA_BLOCK), BF16),
                        pltpu.VMEM((2, nb + HALO, MOBA_BLOCK), F32),
                        pltpu.VMEM((2, 2, 2 * MOBA_BLOCK, MOBA_BLOCK), F32)],
        compiler_params=pltpu.CompilerParams(dimension_semantics=("parallel", "parallel"),
                                             vmem_limit_bytes=VMEM_LIMIT),
        name="moba_attention",
    )(far_bias, zqkv, zqkv, zqkv, bias_tiles)


def _mlstm_kernel(zb_ref, zg_ref, cw_ref, cb_ref, gb_ref, ng_ref, o_ref, hist_ref, state_ref, m_ref, x_ref):
    @pl.when(pl.program_id(1) == 0)
    def _():
        hist_ref[...] = jnp.zeros_like(hist_ref)
        state_ref[...] = jnp.zeros_like(state_ref)
        m_ref[...] = jnp.zeros_like(m_ref)

    refs = (zb_ref, zg_ref, cw_ref, cb_ref, gb_ref, ng_ref, o_ref, hist_ref, state_ref, m_ref, x_ref)
    lax.fori_loop(0, zb_ref.shape[1] // CHUNK, functools.partial(_mlstm_chunk, refs), 0)


def _mlstm_chunk(refs, j, _):
    zb_ref, zg_ref, cw_ref, cb_ref, gb_ref, ng_ref, o_ref, hist_ref, state_ref, m_ref, x_ref = refs
    L = CHUNK
    dh = B_HEAD_DIM
    rows = pl.ds(pl.multiple_of(j * L, L), L)

    x = zb_ref[0, rows, 0:2 * B_WIDTH].astype(F32)
    lanes = x_ref.shape[2]
    ys = []
    for k in range(x_ref.shape[0]):
        ls = slice(k * lanes, (k + 1) * lanes)
        x_ref[k, 0:HALO, :] = hist_ref[:, ls]
        x_ref[k, HALO:, :] = x[:, ls]
        yk = cw_ref[MLSTM_CONV - 1:MLSTM_CONV, ls] * x[:, ls]
        for back in range(1, MLSTM_CONV):
            yk = yk + cw_ref[MLSTM_CONV - 1 - back:MLSTM_CONV - back, ls] * x_ref[k, HALO - back:HALO - back + L, :]
        ys.append(yk)
    hist_ref[...] = x[L - HALO:, :]
    y = jnp.concatenate(ys, axis=1) + cb_ref[...]
    y = y * jax.nn.sigmoid(y)
    q_all = y[:, :B_WIDTH]
    k_all = y[:, B_WIDTH:] * (dh ** -0.5)

    g = zg_ref[0, rows, :] + gb_ref[...]
    lane = lax.broadcasted_iota(I32, (1, 128), 1)
    log_f = jnp.minimum(g, 0.0) - jnp.log(1.0 + jnp.exp(-jnp.abs(g)))
    gates = jnp.where(lane < B_HEADS, g, log_f)
    row = lax.broadcasted_iota(I32, (L, L), 0)
    col = lax.broadcasted_iota(I32, (L, L), 1)
    causal = row >= col
    tril = jnp.where(causal, 1.0, 0.0).astype(BF16)
    triu = jnp.where(row <= col, 1.0, 0.0).astype(BF16)
    csum_col = sum(_dot(tril, part) for part in _split3(gates))
    gates_t = gates.T
    csum_row = sum(_dot(part, triu) for part in _split3(gates_t[0:16, :]))
    top_row = lax.broadcasted_iota(I32, (dh, L), 0) == 0

    heads = range(B_HEADS)
    sls = [slice(h * dh, (h + 1) * dh) for h in heads]
    qhs = [q_all[:, sls[h]].astype(BF16) for h in heads]
    khs = [k_all[:, sls[h]].astype(BF16) for h in heads]
    states = [state_ref[h] for h in heads]
    m_prevs = [m_ref[h] for h in heads]
    qk_t = [_dot_nt(khs[h], qhs[h]) for h in heads]
    inter_t = [_dot_nt(states[h].astype(BF16), qhs[h]) for h in heads]
    v_t = [zb_ref[0, rows, 2 * B_WIDTH + h * dh:2 * B_WIDTH + (h + 1) * dh].astype(F32).T for h in heads]
    ones_t = jnp.where(top_row, 1.0, 0.0).astype(BF16)
    vaug_t = [jnp.concatenate([v_t[h].astype(BF16), ones_t], axis=0) for h in heads]

    c_cols = [gates[:, h:h + 1] - csum_col[:, B_HEADS + h:B_HEADS + h + 1] for h in heads]
    b_rows = [csum_row[B_HEADS + h:B_HEADS + h + 1, :] for h in heads]
    i_rows = [gates_t[h:h + 1, :] for h in heads]
    log_inter = [b_rows[h] + m_prevs[h] for h in heads]
    log_intra = [jnp.where(row <= col, b_rows[h] + c_cols[h], -jnp.inf) for h in heads]
    m_t = [jnp.maximum(log_inter[h], jnp.max(log_intra[h], axis=0, keepdims=True)) for h in heads]
    decay = [jnp.exp(log_intra[h] - m_t[h]) for h in heads]
    w_inter = [jnp.exp(log_inter[h] - m_t[h]) for h in heads]

    nd = [w_inter[h] * inter_t[h] + _dot(vaug_t[h], (qk_t[h] * decay[h]).astype(BF16)) for h in heads]

    b_last = [b_rows[h][:, L - 1:L] for h in heads]
    log_s = [b_last[h] - b_rows[h] + i_rows[h] for h in heads]
    m_new = [jnp.maximum(b_last[h] + m_prevs[h], jnp.max(log_s[h], axis=1, keepdims=True)) for h in heads]
    w_prev = [jnp.exp(b_last[h] + m_prevs[h] - m_new[h]) for h in heads]
    w_s = [jnp.exp(log_s[h] - m_new[h]) for h in heads]
    for h in heads:
        weighted = jnp.concatenate([v_t[h] * w_s[h], jnp.where(top_row, w_s[h], 0.0)], axis=0).astype(BF16)
        state_ref[h] = w_prev[h] * states[h] + _dot(weighted, khs[h])
        m_ref[h] = m_new[h]

    for h in heads:
        den = nd[h][dh:dh + 1, :]
        hout = nd[h][:dh, :] / jnp.maximum(jnp.abs(den), jnp.exp(-m_t[h]))
        hn = hout * lax.rsqrt(jnp.mean(hout * hout, axis=0, keepdims=True) + EPS) * ng_ref[sls[h], :]
        og = jax.nn.sigmoid(zb_ref[0, rows, 3 * B_WIDTH + h * dh:3 * B_WIDTH + (h + 1) * dh].astype(F32))
        o_ref[0, rows, sls[h]] = (hn.T * og).astype(o_ref.dtype)
    return 0


def _mlstm(zb, zg, conv_w, conv_b, b_i, b_f, norm_g):
    bsz, s, _ = zb.shape
    rows = min(MIXER_ROWS, s)
    gate_bias = jnp.zeros((1, 128), F32).at[0, :B_HEADS].set(b_i.astype(F32)).at[0, B_HEADS:2 * B_HEADS].set(
        b_f.astype(F32))
    return pl.pallas_call(
        _mlstm_kernel,
        grid=(bsz, s // rows),
        in_specs=[pl.BlockSpec((1, rows, 4 * B_WIDTH), lambda b, c: (b, c, 0)),
                  pl.BlockSpec((1, rows, 128), lambda b, c: (b, c, 0)),
                  _resident((MLSTM_CONV, 2 * B_WIDTH)), _resident((1, 2 * B_WIDTH)),
                  _resident((1, 128)), _resident((B_WIDTH, 1))],
        out_specs=pl.BlockSpec((1, rows, B_WIDTH), lambda b, c: (b, c, 0)),
        out_shape=jax.ShapeDtypeStruct((bsz, s, B_WIDTH), BF16),
        scratch_shapes=[pltpu.VMEM((HALO, 2 * B_WIDTH), F32),
                        pltpu.VMEM((B_HEADS, 2 * B_HEAD_DIM, B_HEAD_DIM), F32),
                        pltpu.VMEM((B_HEADS, 1, 1), F32),
                        pltpu.VMEM((2 * B_WIDTH // 128, HALO + CHUNK, 128), F32)],
        compiler_params=pltpu.CompilerParams(dimension_semantics=("parallel", "arbitrary"),
                                             vmem_limit_bytes=VMEM_LIMIT),
        name="mlstm",
    )(zb, zg.reshape(bsz, s, 128), conv_w.astype(F32), conv_b.reshape(1, -1).astype(F32), gate_bias,
      norm_g.reshape(-1, 1).astype(F32))


def _pool_ret_kernel(z_ref, cos_ref, sin_ref, swap_ref, dq_ref, dk_ref, dc_ref, dm_ref, bd_ref, pw_ref, ps_ref,
                     rg_ref, o_ref, hist_ref, state_ref, win_ref):
    @pl.when(pl.program_id(1) == 0)
    def _():
        hist_ref[...] = jnp.zeros_like(hist_ref)
        state_ref[...] = jnp.zeros_like(state_ref)

    refs = (z_ref, cos_ref, sin_ref, swap_ref, dq_ref, dk_ref, dc_ref, dm_ref, bd_ref, pw_ref, ps_ref, rg_ref, o_ref,
            hist_ref, state_ref, win_ref)
    lax.fori_loop(0, z_ref.shape[1] // CHUNK, functools.partial(_pool_ret_chunk, refs), 0)


def _pool_ret_chunk(refs, j, _):
    (z_ref, cos_ref, sin_ref, swap_ref, dq_ref, dk_ref, dc_ref, dm_ref, bd_ref, pw_ref, ps_ref, rg_ref, o_ref,
     hist_ref, state_ref, win_ref) = refs
    L = CHUNK
    rows = pl.ds(pl.multiple_of(j * L, L), L)
    c = pl.program_id(1) * (z_ref.shape[1] // L) + j

    u = z_ref[0, rows, 0:POOL_WIDTH].astype(F32)
    pos1 = (c * L + lax.broadcasted_iota(I32, (L, 1), 0) + 1).astype(F32)
    gd = POOL_GROUP_DIM
    for g, w in enumerate(POOL_WINDOWS):
        sl = slice(g * gd, (g + 1) * gd)
        w_ref = win_ref.at[g]
        w_ref[0:POOL_HALO, :] = hist_ref[:, sl]
        w_ref[POOL_HALO:, :] = u[:, sl]
        span = 1
        while span < w:
            n_sum = POOL_HALO + L - span
            w_ref[span:, :] = w_ref[span:span + n_sum, :] + w_ref[0:n_sum, :]
            span *= 2
        pooled = w_ref[POOL_HALO:, :] / jnp.minimum(pos1, float(w)) - u[:, sl]
        o_ref[0, rows, sl] = (_dot(pooled.astype(BF16), pw_ref[g]) * ps_ref[:, sl]).astype(o_ref.dtype)

    rq = z_ref[0, rows, POOL_WIDTH:POOL_WIDTH + R_QK_WIDTH]
    rk = z_ref[0, rows, POOL_WIDTH + R_QK_WIDTH:POOL_WIDTH + 2 * R_QK_WIDTH]
    v0 = POOL_WIDTH + 2 * R_QK_WIDTH
    rv = z_ref[0, rows, v0:v0 + R_V_WIDTH]
    cosv, sinv = cos_ref[rows, :], sin_ref[rows, :]
    q_rot = rq.astype(F32) * cosv + _dot(rq, swap_ref[...]) * sinv
    k_rot = (rk.astype(F32) * cosv + _dot(rk, swap_ref[...]) * sinv) * (R_QK_DIM ** -0.5)
    qb = q_rot.astype(BF16)
    kb = k_rot.astype(BF16)
    inter = _dot((q_rot * dq_ref[...]).astype(BF16), state_ref[...].astype(BF16))
    lane = lax.broadcasted_iota(I32, (1, R_QK_WIDTH), 1)
    g0 = v0 + R_V_WIDTH
    for h in range(R_HEADS):
        in_head = (lane >= h * R_QK_DIM) & (lane < (h + 1) * R_QK_DIM)
        sc = _dot_nt(jnp.where(in_head, qb, jnp.zeros_like(qb)), kb) * dm_ref[h]
        vs = slice(h * R_V_DIM, (h + 1) * R_V_DIM)
        y = _dot(sc.astype(BF16), rv[:, vs]) + inter[:, vs]
        gate = z_ref[0, rows, g0 + h * R_V_DIM:g0 + (h + 1) * R_V_DIM].astype(F32)
        gate = gate * jax.nn.sigmoid(gate)
        o_ref[0, rows, POOL_WIDTH + h * R_V_DIM:POOL_WIDTH + (h + 1) * R_V_DIM] = (
            _rms(y, rg_ref[:, vs]) * gate).astype(o_ref.dtype)
    hist_ref[...] = u[L - POOL_HALO:, :]
    upd = _dot((k_rot * dk_ref[...]).T.astype(BF16), rv)
    state_ref[...] = dc_ref[...] * state_ref[...] + upd * bd_ref[...]
    return 0


def _retention_tables(s):
    L = CHUNK
    half = R_QK_DIM // 2
    inv = ROPE_BASE ** (-jnp.arange(half, dtype=F32) / half)
    ang = jnp.arange(s, dtype=F32)[:, None] * inv[None, :]
    cos, sin = jnp.cos(ang), jnp.sin(ang)
    cos_t = jnp.tile(jnp.concatenate([cos, cos], axis=-1), (1, R_HEADS))
    sin_t = jnp.tile(jnp.concatenate([-sin, sin], axis=-1), (1, R_HEADS))
    j = np.arange(R_QK_WIDTH)
    src = (j // R_QK_DIM) * R_QK_DIM + (j % R_QK_DIM + half) % R_QK_DIM
    swap = np.zeros((R_QK_WIDTH, R_QK_WIDTH), np.float32)
    swap[src, j] = 1.0
    log_g = jnp.log(1.0 - jnp.exp2(-5.0 - jnp.arange(R_HEADS, dtype=F32)))
    t = jnp.arange(L, dtype=F32)
    rel = t[:, None] - t[None, :]
    decay_intra = jnp.where(rel >= 0, jnp.exp(jnp.maximum(rel, 0.0) * log_g[:, None, None]), 0.0)
    decay_q = jnp.exp((t + 1.0) * log_g[:, None])
    decay_k = jnp.exp((L - 1.0 - t) * log_g[:, None])
    decay_chunk = jnp.exp(L * log_g)
    dq = jnp.repeat(decay_q.T, R_QK_DIM, axis=1)
    dk = jnp.repeat(decay_k.T, R_QK_DIM, axis=1)
    dc = jnp.repeat(decay_chunk, R_QK_DIM)[:, None]
    bd = (np.arange(R_QK_WIDTH)[:, None] // R_QK_DIM == np.arange(R_V_WIDTH)[None, :] // R_V_DIM)
    return cos_t, sin_t, jnp.asarray(swap, BF16), dq, dk, dc, decay_intra, jnp.asarray(bd, F32)


def _pool_ret(z, pool_w, pool_scale, ret_g):
    bsz, s, width = z.shape
    cos_t, sin_t, swap, dq, dk, dc, dm, bd = _retention_tables(s)
    rows = min(MIXER_ROWS, s)
    return pl.pallas_call(
        _pool_ret_kernel,
        grid=(bsz, s // rows),
        in_specs=[pl.BlockSpec((1, rows, width), lambda b, c: (b, c, 0)),
                  pl.BlockSpec((rows, R_QK_WIDTH), lambda b, c: (c, 0)),
                  pl.BlockSpec((rows, R_QK_WIDTH), lambda b, c: (c, 0)),
                  _resident(swap.shape), _resident(dq.shape), _resident(dk.shape), _resident(dc.shape),
                  _resident(dm.shape), _resident(bd.shape), _resident(pool_w.shape),
                  _resident((1, POOL_WIDTH)), _resident((1, R_V_WIDTH))],
        out_specs=pl.BlockSpec((1, rows, POOL_WIDTH + R_V_WIDTH), lambda b, c: (b, c, 0)),
        out_shape=jax.ShapeDtypeStruct((bsz, s, POOL_WIDTH + R_V_WIDTH), BF16),
        scratch_shapes=[pltpu.VMEM((POOL_HALO, POOL_WIDTH), F32),
                        pltpu.VMEM((R_QK_WIDTH, R_V_WIDTH), F32),
                        pltpu.VMEM((len(POOL_WINDOWS), POOL_HALO + CHUNK, POOL_GROUP_DIM), F32)],
        compiler_params=pltpu.CompilerParams(dimension_semantics=("parallel", "arbitrary"),
                                             vmem_limit_bytes=VMEM_LIMIT),
        name="pool_retention",
    )(z, cos_t, sin_t, swap, dq, dk, dc, dm, bd, pool_w.astype(BF16), pool_scale.reshape(1, -1).astype(F32),
      ret_g.reshape(1, -1).astype(F32))


def _ffn_kernel(h_ref, a1_ref, a2_ref, wo_ref, g_ref, wup_ref, cw_ref, cb_ref, wd_ref, gout_ref, o_ref,
                hist_ref, y_ref, x_ref, a_ref, *, n_tiles, final_norm):
    tm = h_ref.shape[1]
    lanes = x_ref.shape[2]
    n_lane_tiles = FF_TILE // lanes

    @pl.when(pl.program_id(1) == 0)
    def _():
        hist_ref[...] = jnp.zeros_like(hist_ref)

    half = a1_ref.shape[2]
    h1 = h_ref[0] + _dot(a1_ref[0], wo_ref[0:half, :]) + _dot(a2_ref[0], wo_ref[half:, :])
    y_ref[...] = _rms(h1, g_ref[...]).astype(BF16)
    o_ref[0] = h1

    def cols(idx):
        return pl.ds(pl.multiple_of(idx * FF_TILE, FF_TILE), FF_TILE)

    def conv(u, idx, slot):
        w = cw_ref[:, cols(idx)]
        outs = []
        for k in range(n_lane_tiles):
            ls = slice(k * lanes, (k + 1) * lanes)
            xr = x_ref.at[slot * n_lane_tiles + k]
            xr[0:HALO, :] = hist_ref[idx, :, ls]
            xr[HALO:, :] = u[:, ls]
            hist_ref[idx, :, ls] = xr[tm:tm + HALO, :]
            outs.append(w[2:3, ls] * xr[HALO:HALO + tm, :] + w[1:2, ls] * xr[HALO - 1:HALO - 1 + tm, :]
                        + w[0:1, ls] * xr[HALO - 2:HALO - 2 + tm, :])
        return jnp.concatenate(outs, axis=1) + cb_ref[:, cols(idx)]

    def tile(c, _):
        y = y_ref[...]
        slot = 2 * lax.rem(c, 2)
        gate = conv(_dot(y, wup_ref[:, cols(c)]), c, slot)
        val = conv(_dot(y, wup_ref[:, cols(n_tiles + c)]), n_tiles + c, slot + 1)
        a_ref[:, cols(c)] = (gate * jax.nn.sigmoid(gate) * val).astype(BF16)
        return 0

    lax.fori_loop(0, n_tiles, tile, 0, unroll=2)
    o_ref[0] += _dot(a_ref[...], wd_ref[...])
    if final_norm:
        o_ref[0] = _rms(o_ref[0], gout_ref[...])


def _ffn(h, a1, a2, col1, col2, layer, w_out, norm_g, w_up, conv_w, conv_b, w_down, out_g, final_norm):
    bsz, s, d = h.shape
    tm = min(FFN_ROWS, s)
    half = w_out.shape[0] // 2
    n_tiles = D_FF // FF_TILE
    wo = w_out.astype(BF16)
    g = norm_g.astype(F32).reshape(norm_g.shape[0], 1, d)
    wup = w_up.astype(BF16)
    cw = conv_w.astype(F32)
    cb = conv_b.astype(F32).reshape(conv_b.shape[0], 1, -1)
    wd = w_down.astype(BF16)
    return pl.pallas_call(
        functools.partial(_ffn_kernel, n_tiles=n_tiles, final_norm=final_norm),
        grid=(bsz, s // tm),
        in_specs=[pl.BlockSpec((1, tm, d), lambda b, t: (b, t, 0)),
                  pl.BlockSpec((1, tm, half), lambda b, t: (b, t, col1)),
                  pl.BlockSpec((1, tm, half), lambda b, t: (b, t, col2)),
                  _resident((2 * half, d)), _resident_layer(g, layer),
                  _resident_layer(wup, layer), _resident_layer(cw, layer), _resident_layer(cb, layer),
                  _resident_layer(wd, layer), _resident((1, d))],
        out_specs=pl.BlockSpec((1, tm, d), lambda b, t: (b, t, 0)),
        out_shape=jax.ShapeDtypeStruct((bsz, s, d), F32),
        scratch_shapes=[pltpu.VMEM((2 * n_tiles, HALO, FF_TILE), F32),
                        pltpu.VMEM((tm, d), BF16),
                        pltpu.VMEM((4 * FF_TILE // 128, HALO + tm, 128), F32),
                        pltpu.VMEM((tm, D_FF), BF16)],
        compiler_params=pltpu.CompilerParams(dimension_semantics=("parallel", "arbitrary"),
                                             vmem_limit_bytes=VMEM_LIMIT),
        name="out_proj_ffn",
    )(h, a1, a2, wo, g, wup, cw, cb, wd, out_g.reshape(1, d).astype(F32))


def kernel(x, rel_bias, norm_mix_g, norm_ffn_g, norm_out_g, w_in_ab, w_out_ab, mlstm_conv_w, mlstm_conv_b, mlstm_b_i,
           mlstm_b_f, mlstm_norm_g, w_in_cd, w_out_cd, pool_w, pool_scale, ret_norm_g, ffn_w_up, ffn_conv_w,
           ffn_conv_b, ffn_w_down):
    bsz, s, d = x.shape
    m = bsz * s

    n_attn = 3 * A_WIDTH
    n_main = n_attn + 4 * B_WIDTH
    n_gate = 128
    w0 = jnp.pad(w_in_ab[0].astype(BF16), ((0, 0), (0, n_main + n_gate - w_in_ab.shape[2])))
    zqkv, zb, zg = _norm_proj(x.reshape(m, d), norm_mix_g[0], w0, [n_attn, n_main - n_attn, n_gate],
                              [BF16, BF16, F32])
    bias_tiles = _bias_tables(rel_bias)
    far_bias = rel_bias[REL_BUCKETS - 1].astype(F32)
    ya = _moba(zqkv.reshape(bsz, s, n_attn), bias_tiles, far_bias)
    yb = _mlstm(zb.reshape(bsz, s, 4 * B_WIDTH), zg, mlstm_conv_w[0], mlstm_conv_b[0], mlstm_b_i[0], mlstm_b_f[0],
                mlstm_norm_g[0])
    h = _ffn(x, ya, yb, 0, 0, 0, w_out_ab[0], norm_ffn_g, ffn_w_up, ffn_conv_w, ffn_conv_b, ffn_w_down, norm_out_g, False)

    (z1,) = _norm_proj(h.reshape(m, d), norm_mix_g[1], w_in_cd[0].astype(BF16), [w_in_cd.shape[2]], [BF16])
    cat = _pool_ret(z1.reshape(bsz, s, -1), pool_w[0], pool_scale[0], ret_norm_g[0])
    return _ffn(h, cat, cat, 0, 1, 1, w_out_cd[0], norm_ffn_g, ffn_w_up, ffn_conv_w, ffn_conv_b, ffn_w_down, norm_out_g,
                True)
```

```python
import functools
import math

import numpy as np
import jax
import jax.numpy as jnp
from jax import lax
from jax.experimental import pallas as pl
from jax.experimental.pallas import tpu as pltpu

F32 = jnp.float32
BF16 = jnp.bfloat16
I32 = jnp.int32

EPS = 1e-6
NEG = -1e30

D_MODEL = 1024
A_HEADS = 8
A_HEAD_DIM = 64
A_WIDTH = 512
MOBA_BLOCK = 256
MOBA_TOPK = 3
REL_BUCKETS = 32
REL_MAX_DIST = 128
B_HEADS = 4
B_HEAD_DIM = 128
B_WIDTH = 512
MLSTM_CONV = 4
POOL_WINDOWS = (2, 4, 8, 16)
POOL_GROUP_DIM = 128
POOL_WIDTH = 512
R_HEADS = 4
R_QK_DIM = 64
R_V_DIM = 128
R_QK_WIDTH = 256
R_V_WIDTH = 512
ROPE_BASE = 10000.0
D_FF = 2816
FFN_CONV = 3

CHUNK = 256
ONES_ROWS = 16
HALO = 8
POOL_HALO = 16
FF_TILE = 256
PROJ_ROWS = 1024
FFN_ROWS = 1024
MIXER_ROWS = 1024
VMEM_LIMIT = 60 * 1024 * 1024

_NT = (((1,), (1,)), ((), ()))


def _dot(a, b):
    return jnp.dot(a, b, preferred_element_type=F32)


def _dot_nt(a, b):
    return lax.dot_general(a, b, _NT, preferred_element_type=F32)


def _split3(a):
    a1 = a.astype(BF16)
    r = a - a1.astype(F32)
    a2 = r.astype(BF16)
    a3 = (r - a2.astype(F32)).astype(BF16)
    return a1, a2, a3


def _rms(x, g):
    return x * lax.rsqrt(jnp.mean(x * x, axis=-1, keepdims=True) + EPS) * g


def _resident(shape):
    n = len(shape)
    return pl.BlockSpec(shape, lambda *_: (0,) * n, pipeline_mode=pl.Buffered(1))


def _resident_layer(stacked, layer):
    n = stacked.ndim - 1
    return pl.BlockSpec((None,) + stacked.shape[1:], lambda *_: (layer,) + (0,) * n, pipeline_mode=pl.Buffered(1))


def _bucket_thresholds():
    max_exact = REL_BUCKETS // 2
    d = np.arange(1, 4 * REL_MAX_DIST, dtype=np.float32)
    large = max_exact + (np.log(d / np.float32(max_exact)) / np.float32(math.log(REL_MAX_DIST / max_exact))
                         * np.float32(REL_BUCKETS - max_exact)).astype(np.int32)
    large = np.minimum(large, REL_BUCKETS - 1)
    return [int(d[np.argmax(large >= b)]) for b in range(max_exact + 1, REL_BUCKETS)]


def _bias_kernel(tab_ref, o_ref, *, thresholds):
    h = pl.program_id(0)
    row = lax.broadcasted_iota(I32, (MOBA_BLOCK, MOBA_BLOCK), 0)
    col = lax.broadcasted_iota(I32, (MOBA_BLOCK, MOBA_BLOCK), 1)
    max_exact = REL_BUCKETS // 2
    for t, off in enumerate((0, MOBA_BLOCK)):
        dist = col - row + off
        d = jnp.maximum(dist, 0)
        large = jnp.full(d.shape, max_exact, I32)
        for thr in thresholds:
            large = large + jnp.where(d >= thr, 1, 0)
        bucket = jnp.where(d < max_exact, d, large)
        val = jnp.zeros(d.shape, F32)
        for i in range(REL_BUCKETS):
            val = jnp.where(bucket == i, tab_ref[h, i], val)
        if off == 0:
            val = jnp.where(dist >= 0, val, NEG)
        o_ref[0, t] = val


def _bias_tables(rel_bias):
    tab = rel_bias.astype(F32).T
    return pl.pallas_call(
        functools.partial(_bias_kernel, thresholds=_bucket_thresholds()),
        grid=(A_HEADS,),
        in_specs=[pl.BlockSpec(memory_space=pltpu.SMEM)],
        out_specs=pl.BlockSpec((1, 2, MOBA_BLOCK, MOBA_BLOCK), lambda h: (h, 0, 0, 0)),
        out_shape=jax.ShapeDtypeStruct((A_HEADS, 2, MOBA_BLOCK, MOBA_BLOCK), F32),
        name="rel_bias_tiles",
    )(tab)


def _norm_proj_kernel(h_ref, g_ref, w_ref, *o_refs, col_tile):
    y = _rms(h_ref[...], g_ref[...]).astype(BF16)
    base = 0
    for o_ref in o_refs:
        n = o_ref.shape[1]
        for c0 in range(0, n, col_tile):
            c1 = min(c0 + col_tile, n)
            o_ref[:, c0:c1] = _dot(y, w_ref[:, base + c0:base + c1]).astype(o_ref.dtype)
        base += n


def _norm_proj(h2d, g, w, out_widths, out_dtypes):
    m, d = h2d.shape
    tm = min(PROJ_ROWS, m)
    assert sum(out_widths) == w.shape[1]
    return pl.pallas_call(
        functools.partial(_norm_proj_kernel, col_tile=512),
        grid=(m // tm,),
        in_specs=[pl.BlockSpec((tm, d), lambda i: (i, 0)), _resident((1, d)), _resident(w.shape)],
        out_specs=[pl.BlockSpec((tm, n), lambda i: (i, 0)) for n in out_widths],
        out_shape=[jax.ShapeDtypeStruct((m, n), dt) for n, dt in zip(out_widths, out_dtypes)],
        compiler_params=pltpu.CompilerParams(dimension_semantics=("parallel",), vmem_limit_bytes=VMEM_LIMIT),
        name="norm_in_proj",
    )(h2d, g.reshape(1, d).astype(F32), w)


def _moba_kernel(far_ref, q_ref, k_ref, v_ref, bias_ref, o_ref, kmean_ref, vt_ref, addm_ref, s_ref, *, nb):
    blk = MOBA_BLOCK
    hd = A_HEAD_DIM
    hp = pl.program_id(1)

    for j in range(nb):
        kb = k_ref[0, j * blk:(j + 1) * blk, :].astype(F32)
        kmean_ref[j:j + 1, :] = jnp.sum(kb, axis=0, keepdims=True) * (1.0 / blk)
        vt = v_ref[0, j * blk:(j + 1) * blk, :].astype(F32).T.astype(BF16)
        for hh in range(2):
            vt_ref[j, hh, 0:hd, :] = vt[hh * hd:(hh + 1) * hd, :]
            vt_ref[j, hh, hd:, :] = jnp.ones((ONES_ROWS, blk), BF16)
    addm_ref[:, nb:, :] = jnp.full((2, HALO, blk), NEG, F32)
    addm_ref[:, nb + 1:nb + 2, :] = jnp.zeros((2, 1, blk), F32)

    refs = (far_ref, q_ref, k_ref, bias_ref, o_ref, kmean_ref, vt_ref, addm_ref, s_ref)
    lax.fori_loop(0, nb, functools.partial(_moba_q_block, refs, hp, nb), 0)


def _moba_q_block(refs, hp, nb, qi, _):
    far_ref, q_ref, k_ref, bias_ref, o_ref, kmean_ref, vt_ref, addm_ref, s_ref = refs
    blk = MOBA_BLOCK
    hd = A_HEAD_DIM
    q_rows = pl.ds(pl.multiple_of(qi * blk, blk), blk)
    q = q_ref[0, q_rows, :]
    lane = lax.broadcasted_iota(I32, (1, 2 * hd), 1)
    rowj = lax.broadcasted_iota(I32, (nb, blk), 0)
    scale = hd ** -0.5
    qms = []
    for hh in range(2):
        in_head = (lane >= hh * hd) & (lane < (hh + 1) * hd)
        qm = jnp.where(in_head, q, jnp.zeros_like(q)) * scale
        qms.append(qm)
        km = jnp.where(in_head, kmean_ref[...], 0.0)
        gate = sum(_dot_nt(part, qm) for part in _split3(km))
        gate = jnp.where(rowj < qi, gate, -jnp.inf)
        ranks = []
        for j in range(nb):
            gj = gate[j:j + 1, :]
            ahead = jnp.where(gate > gj, 1.0, 0.0)
            if j > 0:
                ahead = ahead + jnp.where(rowj < j, jnp.where(gate == gj, 1.0, 0.0), 0.0)
            ranks.append(jnp.sum(ahead, axis=0, keepdims=True))
        rank = jnp.concatenate(ranks, axis=0)
        chosen_bias = jnp.where(rowj == qi - 1, 0.0, far_ref[hp * 2 + hh])
        addm_ref[hh, 0:nb, :] = jnp.where(rowj < qi, jnp.where(rank < float(MOBA_TOPK), chosen_bias, NEG), NEG)

    def k_rows(start, rows):
        return k_ref[0, pl.ds(pl.multiple_of(start, blk), rows), :]

    def item(i, slot, carry):
        t = jnp.minimum(i, nb // 2 - 1)
        kb = k_rows(2 * t * blk, 2 * blk)
        ss = [_dot_nt(kb, qms[hh]) for hh in range(2)]
        blocks = (jnp.where(i == 0, pc, 2 * i - 2), jnp.where(i == 0, qi, 2 * i - 1))
        add_rows = (jnp.where(i == 0, nb + 1, 2 * i - 2),
                    jnp.where(i == 0, nb + 1, jnp.where(2 * i - 1 <= qi - 2, 2 * i - 1, nb)))
        adds = [[addm_ref[hh, pl.ds(r, 1), :] for r in add_rows] for hh in range(2)]
        halves = [[s_ref[slot, hh, n * blk:(n + 1) * blk, :] for n in range(2)] for hh in range(2)]
        m_new = [functools.reduce(jnp.maximum, [carry[2 * hh]] + [
            jnp.max(halves[hh][n], axis=0, keepdims=True) + adds[hh][n] for n in range(2)]) for hh in range(2)]
        ps = [[jnp.exp(halves[hh][n] - (m_new[hh] - adds[hh][n])).astype(BF16) for n in range(2)] for hh in range(2)]
        pv = [sum(_dot(vt_ref[j, hh], ps[hh][n]) for n, j in enumerate(blocks))
              for hh in range(2)]
        out = ()
        for hh in range(2):
            alpha = jnp.exp(carry[2 * hh] - m_new[hh])
            out += (m_new[hh], alpha * carry[2 * hh + 1] + pv[hh])
        for hh in range(2):
            s_ref[1 - slot, hh] = ss[hh]
        return out

    pc = jnp.maximum(qi - 1, 0)
    k_prev, k_own = k_rows(pc * blk, blk), k_rows(qi * blk, blk)
    for hh in range(2):
        s_ref[0, hh, 0:blk, :] = _dot_nt(k_prev, qms[hh]) + bias_ref[hh, 1] + addm_ref[hh, pl.ds(pc, 1), :]
        s_ref[0, hh, blk:, :] = _dot_nt(k_own, qms[hh]) + bias_ref[hh, 0]
    n_items = jnp.maximum(qi, 1) // 2 + 1

    start = (jnp.full((1, blk), -jnp.inf, F32), jnp.zeros((hd + ONES_ROWS, blk), F32))
    carry = lax.fori_loop(0, n_items // 2, lambda t, c: item(2 * t + 1, 1, item(2 * t, 0, c)), start + start)
    carry = lax.cond(lax.rem(n_items, 2) == 1, lambda c: item(n_items - 1, 0, c), lambda c: c, carry)
    outs = [carry[2 * hh + 1][:hd] / carry[2 * hh + 1][hd:hd + 1] for hh in range(2)]
    o_ref[0, q_rows, :] = jnp.concatenate(outs, axis=0).T.astype(o_ref.dtype)
    return 0


def _moba(zqkv, bias_tiles, far_bias):
    bsz, s, _ = zqkv.shape
    nb = s // MOBA_BLOCK
    n_pairs = A_HEADS // 2
    pair_w = 2 * A_HEAD_DIM
    return pl.pallas_call(
        functools.partial(_moba_kernel, nb=nb),
        grid=(bsz, n_pairs),
        in_specs=[pl.BlockSpec(memory_space=pltpu.SMEM),
                  pl.BlockSpec((1, s, pair_w), lambda b, hp: (b, 0, hp)),
                  pl.BlockSpec((1, s, pair_w), lambda b, hp: (b, 0, n_pairs + hp)),
                  pl.BlockSpec((1, s, pair_w), lambda b, hp: (b, 0, 2 * n_pairs + hp)),
                  pl.BlockSpec((2, 2, MOBA_BLOCK, MOBA_BLOCK), lambda b, hp: (hp, 0, 0, 0))],
        out_specs=pl.BlockSpec((1, s, pair_w), lambda b, hp: (b, 0, hp)),
        out_shape=jax.ShapeDtypeStruct((bsz, s, A_WIDTH), BF16),
        scratch_shapes=[pltpu.VMEM((nb, pair_w), F32),
                        pltpu.VMEM((nb, 2, A_HEAD_DIM + ONES_ROWS, MOBA_BLOCK), BF16),
                        pltpu.VMEM((2, nb + HALO, MOBA_BLOCK), F32),
                        pltpu.VMEM((2, 2, 2 * MOBA_BLOCK, MOBA_BLOCK), F32)],
        compiler_params=pltpu.CompilerParams(dimension_semantics=("parallel", "parallel"),
                                             vmem_limit_bytes=VMEM_LIMIT),
        name="moba_attention",
    )(far_bias, zqkv, zqkv, zqkv, bias_tiles)


def _mlstm_kernel(zb_ref, zg_ref, cw_ref, cb_ref, gb_ref, ng_ref, o_ref, hist_ref, state_ref, m_ref, x_ref):
    @pl.when(pl.program_id(1) == 0)
    def _():
        hist_ref[...] = jnp.zeros_like(hist_ref)
        state_ref[...] = jnp.zeros_like(state_ref)
        m_ref[...] = jnp.zeros_like(m_ref)

    refs = (zb_ref, zg_ref, cw_ref, cb_ref, gb_ref, ng_ref, o_ref, hist_ref, state_ref, m_ref, x_ref)
    lax.fori_loop(0, zb_ref.shape[1] // CHUNK, functools.partial(_mlstm_chunk, refs), 0)


def _mlstm_chunk(refs, j, _):
    zb_ref, zg_ref, cw_ref, cb_ref, gb_ref, ng_ref, o_ref, hist_ref, state_ref, m_ref, x_ref = refs
    L = CHUNK
    dh = B_HEAD_DIM
    rows = pl.ds(pl.multiple_of(j * L, L), L)

    x = zb_ref[0, rows, 0:2 * B_WIDTH].astype(F32)
    lanes = x_ref.shape[2]
    ys = []
    for k in range(x_ref.shape[0]):
        ls = slice(k * lanes, (k + 1) * lanes)
        x_ref[k, 0:HALO, :] = hist_ref[:, ls]
        x_ref[k, HALO:, :] = x[:, ls]
        yk = cw_ref[MLSTM_CONV - 1:MLSTM_CONV, ls] * x[:, ls]
        for back in range(1, MLSTM_CONV):
            yk = yk + cw_ref[MLSTM_CONV - 1 - back:MLSTM_CONV - back, ls] * x_ref[k, HALO - back:HALO - back + L, :]
        ys.append(yk)
    hist_ref[...] = x[L - HALO:, :]
    y = jnp.concatenate(ys, axis=1) + cb_ref[...]
    y = y * jax.nn.sigmoid(y)
    q_all = y[:, :B_WIDTH]
    k_all = y[:, B_WIDTH:] * (dh ** -0.5)

    g = zg_ref[0, rows, :] + gb_ref[...]
    lane = lax.broadcasted_iota(I32, (1, 128), 1)
    log_f = jnp.minimum(g, 0.0) - jnp.log(1.0 + jnp.exp(-jnp.abs(g)))
    gates = jnp.where(lane < B_HEADS, g, log_f)
    row = lax.broadcasted_iota(I32, (L, L), 0)
    col = lax.broadcasted_iota(I32, (L, L), 1)
    causal = row >= col
    tril = jnp.where(causal, 1.0, 0.0).astype(BF16)
    triu = jnp.where(row <= col, 1.0, 0.0).astype(BF16)
    csum_col = sum(_dot(tril, part) for part in _split3(gates))
    gates_t = gates.T
    csum_row = sum(_dot(part, triu) for part in _split3(gates_t[0:16, :]))
    top_row = lax.broadcasted_iota(I32, (dh, L), 0) == 0

    heads = range(B_HEADS)
    sls = [slice(h * dh, (h + 1) * dh) for h in heads]
    qhs = [q_all[:, sls[h]].astype(BF16) for h in heads]
    khs = [k_all[:, sls[h]].astype(BF16) for h in heads]
    states = [state_ref[h] for h in heads]
    m_prevs = [m_ref[h] for h in heads]
    qk_t = [_dot_nt(khs[h], qhs[h]) for h in heads]
    inter_t = [_dot_nt(states[h].astype(BF16), qhs[h]) for h in heads]
    v_t = [zb_ref[0, rows, 2 * B_WIDTH + h * dh:2 * B_WIDTH + (h + 1) * dh].astype(F32).T for h in heads]
    ones_t = jnp.where(top_row, 1.0, 0.0).astype(BF16)
    vaug_t = [jnp.concatenate([v_t[h].astype(BF16), ones_t], axis=0) for h in heads]

    c_cols = [gates[:, h:h + 1] - csum_col[:, B_HEADS + h:B_HEADS + h + 1] for h in heads]
    b_rows = [csum_row[B_HEADS + h:B_HEADS + h + 1, :] for h in heads]
    i_rows = [gates_t[h:h + 1, :] for h in heads]
    log_inter = [b_rows[h] + m_prevs[h] for h in heads]
    log_intra = [jnp.where(row <= col, b_rows[h] + c_cols[h], -jnp.inf) for h in heads]
    m_t = [jnp.maximum(log_inter[h], jnp.max(log_intra[h], axis=0, keepdims=True)) for h in heads]
    decay = [jnp.exp(log_intra[h] - m_t[h]) for h in heads]
    w_inter = [jnp.exp(log_inter[h] - m_t[h]) for h in heads]

    nd = [w_inter[h] * inter_t[h] + _dot(vaug_t[h], (qk_t[h] * decay[h]).astype(BF16)) for h in heads]

    b_last = [b_rows[h][:, L - 1:L] for h in heads]
    log_s = [b_last[h] - b_rows[h] + i_rows[h] for h in heads]
    m_new = [jnp.maximum(b_last[h] + m_prevs[h], jnp.max(log_s[h], axis=1, keepdims=True)) for h in heads]
    w_prev = [jnp.exp(b_last[h] + m_prevs[h] - m_new[h]) for h in heads]
    w_s = [jnp.exp(log_s[h] - m_new[h]) for h in heads]
    for h in heads:
        weighted = jnp.concatenate([v_t[h] * w_s[h], jnp.where(top_row, w_s[h], 0.0)], axis=0).astype(BF16)
        state_ref[h] = w_prev[h] * states[h] + _dot(weighted, khs[h])
        m_ref[h] = m_new[h]

    for h in heads:
        den = nd[h][dh:dh + 1, :]
        hout = nd[h][:dh, :] / jnp.maximum(jnp.abs(den), jnp.exp(-m_t[h]))
        hn = hout * lax.rsqrt(jnp.mean(hout * hout, axis=0, keepdims=True) + EPS) * ng_ref[sls[h], :]
        og = jax.nn.sigmoid(zb_ref[0, rows, 3 * B_WIDTH + h * dh:3 * B_WIDTH + (h + 1) * dh].astype(F32))
        o_ref[0, rows, sls[h]] = (hn.T * og).astype(o_ref.dtype)
    return 0


def _mlstm(zb, zg, conv_w, conv_b, b_i, b_f, norm_g):
    bsz, s, _ = zb.shape
    rows = min(MIXER_ROWS, s)
    gate_bias = jnp.zeros((1, 128), F32).at[0, :B_HEADS].set(b_i.astype(F32)).at[0, B_HEADS:2 * B_HEADS].set(
        b_f.astype(F32))
    return pl.pallas_call(
        _mlstm_kernel,
        grid=(bsz, s // rows),
        in_specs=[pl.BlockSpec((1, rows, 4 * B_WIDTH), lambda b, c: (b, c, 0)),
                  pl.BlockSpec((1, rows, 128), lambda b, c: (b, c, 0)),
                  _resident((MLSTM_CONV, 2 * B_WIDTH)), _resident((1, 2 * B_WIDTH)),
                  _resident((1, 128)), _resident((B_WIDTH, 1))],
        out_specs=pl.BlockSpec((1, rows, B_WIDTH), lambda b, c: (b, c, 0)),
        out_shape=jax.ShapeDtypeStruct((bsz, s, B_WIDTH), BF16),
        scratch_shapes=[pltpu.VMEM((HALO, 2 * B_WIDTH), F32),
                        pltpu.VMEM((B_HEADS, 2 * B_HEAD_DIM, B_HEAD_DIM), F32),
                        pltpu.VMEM((B_HEADS, 1, 1), F32),
                        pltpu.VMEM((2 * B_WIDTH // 128, HALO + CHUNK, 128), F32)],
        compiler_params=pltpu.CompilerParams(dimension_semantics=("parallel", "arbitrary"),
                                             vmem_limit_bytes=VMEM_LIMIT),
        name="mlstm",
    )(zb, zg.reshape(bsz, s, 128), conv_w.astype(F32), conv_b.reshape(1, -1).astype(F32), gate_bias,
      norm_g.reshape(-1, 1).astype(F32))


def _pool_ret_kernel(z_ref, cos_ref, sin_ref, swap_ref, dq_ref, dk_ref, dc_ref, dm_ref, bd_ref, pw_ref, ps_ref,
                     rg_ref, o_ref, hist_ref, state_ref, win_ref):
    @pl.when(pl.program_id(1) == 0)
    def _():
        hist_ref[...] = jnp.zeros_like(hist_ref)
        state_ref[...] = jnp.zeros_like(state_ref)

    refs = (z_ref, cos_ref, sin_ref, swap_ref, dq_ref, dk_ref, dc_ref, dm_ref, bd_ref, pw_ref, ps_ref, rg_ref, o_ref,
            hist_ref, state_ref, win_ref)
    lax.fori_loop(0, z_ref.shape[1] // CHUNK, functools.partial(_pool_ret_chunk, refs), 0)


def _pool_ret_chunk(refs, j, _):
    (z_ref, cos_ref, sin_ref, swap_ref, dq_ref, dk_ref, dc_ref, dm_ref, bd_ref, pw_ref, ps_ref, rg_ref, o_ref,
     hist_ref, state_ref, win_ref) = refs
    L = CHUNK
    rows = pl.ds(pl.multiple_of(j * L, L), L)
    c = pl.program_id(1) * (z_ref.shape[1] // L) + j

    u = z_ref[0, rows, 0:POOL_WIDTH].astype(F32)
    pos1 = (c * L + lax.broadcasted_iota(I32, (L, 1), 0) + 1).astype(F32)
    gd = POOL_GROUP_DIM
    for g, w in enumerate(POOL_WINDOWS):
        sl = slice(g * gd, (g + 1) * gd)
        w_ref = win_ref.at[g]
        w_ref[0:POOL_HALO, :] = hist_ref[:, sl]
        w_ref[POOL_HALO:, :] = u[:, sl]
        span = 1
        while span < w:
            n_sum = POOL_HALO + L - span
            w_ref[span:, :] = w_ref[span:span + n_sum, :] + w_ref[0:n_sum, :]
            span *= 2
        pooled = w_ref[POOL_HALO:, :] / jnp.minimum(pos1, float(w)) - u[:, sl]
        o_ref[0, rows, sl] = (_dot(pooled.astype(BF16), pw_ref[g]) * ps_ref[:, sl]).astype(o_ref.dtype)

    rq = z_ref[0, rows, POOL_WIDTH:POOL_WIDTH + R_QK_WIDTH]
    rk = z_ref[0, rows, POOL_WIDTH + R_QK_WIDTH:POOL_WIDTH + 2 * R_QK_WIDTH]
    v0 = POOL_WIDTH + 2 * R_QK_WIDTH
    rv = z_ref[0, rows, v0:v0 + R_V_WIDTH]
    cosv, sinv = cos_ref[rows, :], sin_ref[rows, :]
    q_rot = rq.astype(F32) * cosv + _dot(rq, swap_ref[...]) * sinv
    k_rot = (rk.astype(F32) * cosv + _dot(rk, swap_ref[...]) * sinv) * (R_QK_DIM ** -0.5)
    qb = q_rot.astype(BF16)
    kb = k_rot.astype(BF16)
    inter = _dot((q_rot * dq_ref[...]).astype(BF16), state_ref[...].astype(BF16))
    lane = lax.broadcasted_iota(I32, (1, R_QK_WIDTH), 1)
    g0 = v0 + R_V_WIDTH
    for h in range(R_HEADS):
        in_head = (lane >= h * R_QK_DIM) & (lane < (h + 1) * R_QK_DIM)
        sc = _dot_nt(jnp.where(in_head, qb, jnp.zeros_like(qb)), kb) * dm_ref[h]
        vs = slice(h * R_V_DIM, (h + 1) * R_V_DIM)
        y = _dot(sc.astype(BF16), rv[:, vs]) + inter[:, vs]
        gate = z_ref[0, rows, g0 + h * R_V_DIM:g0 + (h + 1) * R_V_DIM].astype(F32)
        gate = gate * jax.nn.sigmoid(gate)
        o_ref[0, rows, POOL_WIDTH + h * R_V_DIM:POOL_WIDTH + (h + 1) * R_V_DIM] = (
            _rms(y, rg_ref[:, vs]) * gate).astype(o_ref.dtype)
    hist_ref[...] = u[L - POOL_HALO:, :]
    upd = _dot((k_rot * dk_ref[...]).T.astype(BF16), rv)
    state_ref[...] = dc_ref[...] * state_ref[...] + upd * bd_ref[...]
    return 0


def _retention_tables(s):
    L = CHUNK
    half = R_QK_DIM // 2
    inv = ROPE_BASE ** (-jnp.arange(half, dtype=F32) / half)
    ang = jnp.arange(s, dtype=F32)[:, None] * inv[None, :]
    cos, sin = jnp.cos(ang), jnp.sin(ang)
    cos_t = jnp.tile(jnp.concatenate([cos, cos], axis=-1), (1, R_HEADS))
    sin_t = jnp.tile(jnp.concatenate([-sin, sin], axis=-1), (1, R_HEADS))
    j = np.arange(R_QK_WIDTH)
    src = (j // R_QK_DIM) * R_QK_DIM + (j % R_QK_DIM + half) % R_QK_DIM
    swap = np.zeros((R_QK_WIDTH, R_QK_WIDTH), np.float32)
    swap[src, j] = 1.0
    log_g = jnp.log(1.0 - jnp.exp2(-5.0 - jnp.arange(R_HEADS, dtype=F32)))
    t = jnp.arange(L, dtype=F32)
    rel = t[:, None] - t[None, :]
    decay_intra = jnp.where(rel >= 0, jnp.exp(jnp.maximum(rel, 0.0) * log_g[:, None, None]), 0.0)
    decay_q = jnp.exp((t + 1.0) * log_g[:, None])
    decay_k = jnp.exp((L - 1.0 - t) * log_g[:, None])
    decay_chunk = jnp.exp(L * log_g)
    dq = jnp.repeat(decay_q.T, R_QK_DIM, axis=1)
    dk = jnp.repeat(decay_k.T, R_QK_DIM, axis=1)
    dc = jnp.repeat(decay_chunk, R_QK_DIM)[:, None]
    bd = (np.arange(R_QK_WIDTH)[:, None] // R_QK_DIM == np.arange(R_V_WIDTH)[None, :] // R_V_DIM)
    return cos_t, sin_t, jnp.asarray(swap, BF16), dq, dk, dc, decay_intra, jnp.asarray(bd, F32)


def _pool_ret(z, pool_w, pool_scale, ret_g):
    bsz, s, width = z.shape
    cos_t, sin_t, swap, dq, dk, dc, dm, bd = _retention_tables(s)
    rows = min(MIXER_ROWS, s)
    return pl.pallas_call(
        _pool_ret_kernel,
        grid=(bsz, s // rows),
        in_specs=[pl.BlockSpec((1, rows, width), lambda b, c: (b, c, 0)),
                  pl.BlockSpec((rows, R_QK_WIDTH), lambda b, c: (c, 0)),
                  pl.BlockSpec((rows, R_QK_WIDTH), lambda b, c: (c, 0)),
                  _resident(swap.shape), _resident(dq.shape), _resident(dk.shape), _resident(dc.shape),
                  _resident(dm.shape), _resident(bd.shape), _resident(pool_w.shape),
                  _resident((1, POOL_WIDTH)), _resident((1, R_V_WIDTH))],
        out_specs=pl.BlockSpec((1, rows, POOL_WIDTH + R_V_WIDTH), lambda b, c: (b, c, 0)),
        out_shape=jax.ShapeDtypeStruct((bsz, s, POOL_WIDTH + R_V_WIDTH), BF16),
        scratch_shapes=[pltpu.VMEM((POOL_HALO, POOL_WIDTH), F32),
                        pltpu.VMEM((R_QK_WIDTH, R_V_WIDTH), F32),
                        pltpu.VMEM((len(POOL_WINDOWS), POOL_HALO + CHUNK, POOL_GROUP_DIM), F32)],
        compiler_params=pltpu.CompilerParams(dimension_semantics=("parallel", "arbitrary"),
                                             vmem_limit_bytes=VMEM_LIMIT),
        name="pool_retention",
    )(z, cos_t, sin_t, swap, dq, dk, dc, dm, bd, pool_w.astype(BF16), pool_scale.reshape(1, -1).astype(F32),
      ret_g.reshape(1, -1).astype(F32))


def _ffn_kernel(h_ref, a1_ref, a2_ref, wo_ref, g_ref, wup_ref, cw_ref, cb_ref, wd_ref, gout_ref, o_ref,
                hist_ref, y_ref, x_ref, a_ref, *, n_tiles, final_norm):
    tm = h_ref.shape[1]
    lanes = x_ref.shape[2]
    n_lane_tiles = FF_TILE // lanes

    @pl.when(pl.program_id(1) == 0)
    def _():
        hist_ref[...] = jnp.zeros_like(hist_ref)

    half = a1_ref.shape[2]
    h1 = h_ref[0] + _dot(a1_ref[0], wo_ref[0:half, :]) + _dot(a2_ref[0], wo_ref[half:, :])
    y_ref[...] = _rms(h1, g_ref[...]).astype(BF16)
    o_ref[0] = h1

    def cols(idx):
        return pl.ds(pl.multiple_of(idx * FF_TILE, FF_TILE), FF_TILE)

    def conv(u, idx, slot):
        w = cw_ref[:, cols(idx)]
        outs = []
        for k in range(n_lane_tiles):
            ls = slice(k * lanes, (k + 1) * lanes)
            xr = x_ref.at[slot * n_lane_tiles + k]
            xr[0:HALO, :] = hist_ref[idx, :, ls]
            xr[HALO:, :] = u[:, ls]
            hist_ref[idx, :, ls] = xr[tm:tm + HALO, :]
            outs.append(w[2:3, ls] * xr[HALO:HALO + tm, :] + w[1:2, ls] * xr[HALO - 1:HALO - 1 + tm, :]
                        + w[0:1, ls] * xr[HALO - 2:HALO - 2 + tm, :])
        return jnp.concatenate(outs, axis=1) + cb_ref[:, cols(idx)]

    def tile(c, _):
        y = y_ref[...]
        slot = 2 * lax.rem(c, 2)
        gate = conv(_dot(y, wup_ref[:, cols(c)]), c, slot)
        val = conv(_dot(y, wup_ref[:, cols(n_tiles + c)]), n_tiles + c, slot + 1)
        a_ref[:, cols(c)] = (gate * jax.nn.sigmoid(gate) * val).astype(BF16)
        return 0

    lax.fori_loop(0, n_tiles, tile, 0, unroll=2)
    o_ref[0] += _dot(a_ref[...], wd_ref[...])
    if final_norm:
        o_ref[0] = _rms(o_ref[0], gout_ref[...])


def _ffn(h, a1, a2, col1, col2, layer, w_out, norm_g, w_up, conv_w, conv_b, w_down, out_g, final_norm):
    bsz, s, d = h.shape
    tm = min(FFN_ROWS, s)
    half = w_out.shape[0] // 2
    n_tiles = D_FF // FF_TILE
    wo = w_out.astype(BF16)
    g = norm_g.astype(F32).reshape(norm_g.shape[0], 1, d)
    wup = w_up.astype(BF16)
    cw = conv_w.astype(F32)
    cb = conv_b.astype(F32).reshape(conv_b.shape[0], 1, -1)
    wd = w_down.astype(BF16)
    return pl.pallas_call(
        functools.partial(_ffn_kernel, n_tiles=n_tiles, final_norm=final_norm),
        grid=(bsz, s // tm),
        in_specs=[pl.BlockSpec((1, tm, d), lambda b, t: (b, t, 0)),
                  pl.BlockSpec((1, tm, half), lambda b, t: (b, t, col1)),
                  pl.BlockSpec((1, tm, half), lambda b, t: (b, t, col2)),
                  _resident((2 * half, d)), _resident_layer(g, layer),
                  _resident_layer(wup, layer), _resident_layer(cw, layer), _resident_layer(cb, layer),
                  _resident_layer(wd, layer), _resident((1, d))],
        out_specs=pl.BlockSpec((1, tm, d), lambda b, t: (b, t, 0)),
        out_shape=jax.ShapeDtypeStruct((bsz, s, d), F32),
        scratch_shapes=[pltpu.VMEM((2 * n_tiles, HALO, FF_TILE), F32),
                        pltpu.VMEM((tm, d), BF16),
                        pltpu.VMEM((4 * FF_TILE // 128, HALO + tm, 128), F32),
                        pltpu.VMEM((tm, D_FF), BF16)],
        compiler_params=pltpu.CompilerParams(dimension_semantics=("parallel", "arbitrary"),
                                             vmem_limit_bytes=VMEM_LIMIT),
        name="out_proj_ffn",
    )(h, a1, a2, wo, g, wup, cw, cb, wd, out_g.reshape(1, d).astype(F32))


def kernel(x, rel_bias, norm_mix_g, norm_ffn_g, norm_out_g, w_in_ab, w_out_ab, mlstm_conv_w, mlstm_conv_b, mlstm_b_i,
           mlstm_b_f, mlstm_norm_g, w_in_cd, w_out_cd, pool_w, pool_scale, ret_norm_g, ffn_w_up, ffn_conv_w,
           ffn_conv_b, ffn_w_down):
    bsz, s, d = x.shape
    m = bsz * s

    n_attn = 3 * A_WIDTH
    n_main = n_attn + 4 * B_WIDTH
    n_gate = 128
    w0 = jnp.pad(w_in_ab[0].astype(BF16), ((0, 0), (0, n_main + n_gate - w_in_ab.shape[2])))
    zqkv, zb, zg = _norm_proj(x.reshape(m, d), norm_mix_g[0], w0, [n_attn, n_main - n_attn, n_gate],
                              [BF16, BF16, F32])
    bias_tiles = _bias_tables(rel_bias)
    far_bias = rel_bias[REL_BUCKETS - 1].astype(F32)
    ya = _moba(zqkv.reshape(bsz, s, n_attn), bias_tiles, far_bias)
    yb = _mlstm(zb.reshape(bsz, s, 4 * B_WIDTH), zg, mlstm_conv_w[0], mlstm_conv_b[0], mlstm_b_i[0], mlstm_b_f[0],
                mlstm_norm_g[0])
    h = _ffn(x, ya, yb, 0, 0, 0, w_out_ab[0], norm_ffn_g, ffn_w_up, ffn_conv_w, ffn_conv_b, ffn_w_down, norm_out_g, False)

    (z1,) = _norm_proj(h.reshape(m, d), norm_mix_g[1], w_in_cd[0].astype(BF16), [w_in_cd.shape[2]], [BF16])
    cat = _pool_ret(z1.reshape(bsz, s, -1), pool_w[0], pool_scale[0], ret_norm_g[0])
    return _ffn(h, cat, cat, 0, 1, 1, w_out_cd[0], norm_ffn_g, ffn_w_up, ffn_conv_w, ffn_conv_b, ffn_w_down, norm_out_g,
                True)
```

```python
import functools
import math

import numpy as np
import jax
import jax.numpy as jnp
from jax import lax
from jax.experimental import pallas as pl
from jax.experimental.pallas import tpu as pltpu

F32 = jnp.float32
BF16 = jnp.bfloat16
I32 = jnp.int32

EPS = 1e-6
NEG = -1e30

D_MODEL = 1024
A_HEADS = 8
A_HEAD_DIM = 64
A_WIDTH = 512
MOBA_BLOCK = 256
MOBA_TOPK = 3
REL_BUCKETS = 32
REL_MAX_DIST = 128
B_HEADS = 4
B_HEAD_DIM = 128
B_WIDTH = 512
MLSTM_CONV = 4
POOL_WINDOWS = (2, 4, 8, 16)
POOL_GROUP_DIM = 128
POOL_WIDTH = 512
R_HEADS = 4
R_QK_DIM = 64
R_V_DIM = 128
R_QK_WIDTH = 256
R_V_WIDTH = 512
ROPE_BASE = 10000.0
D_FF = 2816
FFN_CONV = 3

CHUNK = 256
ONES_ROWS = 16
HALO = 8
POOL_HALO = 16
FF_TILE = 256
PROJ_ROWS = 1024
FFN_ROWS = 1024
MIXER_ROWS = 1024
VMEM_LIMIT = 60 * 1024 * 1024

_NT = (((1,), (1,)), ((), ()))


def _dot(a, b):
    return jnp.dot(a, b, preferred_element_type=F32)


def _dot_nt(a, b):
    return lax.dot_general(a, b, _NT, preferred_element_type=F32)


def _split3(a):
    a1 = a.astype(BF16)
    r = a - a1.astype(F32)
    a2 = r.astype(BF16)
    a3 = (r - a2.astype(F32)).astype(BF16)
    return a1, a2, a3


def _rms(x, g):
    return x * lax.rsqrt(jnp.mean(x * x, axis=-1, keepdims=True) + EPS) * g


def _resident(shape):
    n = len(shape)
    return pl.BlockSpec(shape, lambda *_: (0,) * n, pipeline_mode=pl.Buffered(1))


def _resident_layer(stacked, layer):
    n = stacked.ndim - 1
    return pl.BlockSpec((None,) + stacked.shape[1:], lambda *_: (layer,) + (0,) * n, pipeline_mode=pl.Buffered(1))


def _bucket_thresholds():
    max_exact = REL_BUCKETS // 2
    d = np.arange(1, 4 * REL_MAX_DIST, dtype=np.float32)
    large = max_exact + (np.log(d / np.float32(max_exact)) / np.float32(math.log(REL_MAX_DIST / max_exact))
                         * np.float32(REL_BUCKETS - max_exact)).astype(np.int32)
    large = np.minimum(large, REL_BUCKETS - 1)
    return [int(d[np.argmax(large >= b)]) for b in range(max_exact + 1, REL_BUCKETS)]


def _bias_kernel(tab_ref, o_ref, *, thresholds):
    h = pl.program_id(0)
    row = lax.broadcasted_iota(I32, (MOBA_BLOCK, MOBA_BLOCK), 0)
    col = lax.broadcasted_iota(I32, (MOBA_BLOCK, MOBA_BLOCK), 1)
    max_exact = REL_BUCKETS // 2
    for t, off in enumerate((0, MOBA_BLOCK)):
        dist = col - row + off
        d = jnp.maximum(dist, 0)
        large = jnp.full(d.shape, max_exact, I32)
        for thr in thresholds:
            large = large + jnp.where(d >= thr, 1, 0)
        bucket = jnp.where(d < max_exact, d, large)
        val = jnp.zeros(d.shape, F32)
        for i in range(REL_BUCKETS):
            val = jnp.where(bucket == i, tab_ref[h, i], val)
        if off == 0:
            val = jnp.where(dist >= 0, val, NEG)
        o_ref[0, t] = val


def _bias_tables(rel_bias):
    tab = rel_bias.astype(F32).T
    return pl.pallas_call(
        functools.partial(_bias_kernel, thresholds=_bucket_thresholds()),
        grid=(A_HEADS,),
        in_specs=[pl.BlockSpec(memory_space=pltpu.SMEM)],
        out_specs=pl.BlockSpec((1, 2, MOBA_BLOCK, MOBA_BLOCK), lambda h: (h, 0, 0, 0)),
        out_shape=jax.ShapeDtypeStruct((A_HEADS, 2, MOBA_BLOCK, MOBA_BLOCK), F32),
        name="rel_bias_tiles",
    )(tab)


def _norm_proj_kernel(h_ref, g_ref, w_ref, *o_refs, col_tile):
    y = _rms(h_ref[...], g_ref[...]).astype(BF16)
    base = 0
    for o_ref in o_refs:
        n = o_ref.shape[1]
        for c0 in range(0, n, col_tile):
            c1 = min(c0 + col_tile, n)
            o_ref[:, c0:c1] = _dot(y, w_ref[:, base + c0:base + c1]).astype(o_ref.dtype)
        base += n


def _norm_proj(h2d, g, w, out_widths, out_dtypes):
    m, d = h2d.shape
    tm = min(PROJ_ROWS, m)
    assert sum(out_widths) == w.shape[1]
    return pl.pallas_call(
        functools.partial(_norm_proj_kernel, col_tile=512),
        grid=(m // tm,),
        in_specs=[pl.BlockSpec((tm, d), lambda i: (i, 0)), _resident((1, d)), _resident(w.shape)],
        out_specs=[pl.BlockSpec((tm, n), lambda i: (i, 0)) for n in out_widths],
        out_shape=[jax.ShapeDtypeStruct((m, n), dt) for n, dt in zip(out_widths, out_dtypes)],
        compiler_params=pltpu.CompilerParams(dimension_semantics=("parallel",), vmem_limit_bytes=VMEM_LIMIT),
        name="norm_in_proj",
    )(h2d, g.reshape(1, d).astype(F32), w)


def _moba_kernel(far_ref, q_ref, k_ref, v_ref, bias_ref, o_ref, kmean_ref, vt_ref, addm_ref, s_ref, *, nb):
    blk = MOBA_BLOCK
    hd = A_HEAD_DIM
    hp = pl.program_id(1)

    for j in range(nb):
        kb = k_ref[0, j * blk:(j + 1) * blk, :].astype(F32)
        kmean_ref[j:j + 1, :] = jnp.sum(kb, axis=0, keepdims=True) * (1.0 / blk)
        vt = v_ref[0, j * blk:(j + 1) * blk, :].astype(F32).T.astype(BF16)
        for hh in range(2):
            vt_ref[j, hh, 0:hd, :] = vt[hh * hd:(hh + 1) * hd, :]
            vt_ref[j, hh, hd:, :] = jnp.ones((ONES_ROWS, blk), BF16)
    addm_ref[:, nb:, :] = jnp.full((2, HALO, blk), NEG, F32)
    addm_ref[:, nb + 1:nb + 2, :] = jnp.zeros((2, 1, blk), F32)

    refs = (far_ref, q_ref, k_ref, bias_ref, o_ref, kmean_ref, vt_ref, addm_ref, s_ref)
    lax.fori_loop(0, nb, functools.partial(_moba_q_block, refs, hp, nb), 0)


def _moba_q_block(refs, hp, nb, qi, _):
    far_ref, q_ref, k_ref, bias_ref, o_ref, kmean_ref, vt_ref, addm_ref, s_ref = refs
    blk = MOBA_BLOCK
    hd = A_HEAD_DIM
    q_rows = pl.ds(pl.multiple_of(qi * blk, blk), blk)
    q = q_ref[0, q_rows, :]
    lane = lax.broadcasted_iota(I32, (1, 2 * hd), 1)
    rowj = lax.broadcasted_iota(I32, (nb, blk), 0)
    scale = hd ** -0.5
    qms = []
    for hh in range(2):
        in_head = (lane >= hh * hd) & (lane < (hh + 1) * hd)
        qm = jnp.where(in_head, q, jnp.zeros_like(q)) * scale
        qms.append(qm)
        km = jnp.where(in_head, kmean_ref[...], 0.0)
        gate = sum(_dot_nt(part, qm) for part in _split3(km))
        gate = jnp.where(rowj < qi, gate, -jnp.inf)
        ranks = []
        for j in range(nb):
            gj = gate[j:j + 1, :]
            ahead = jnp.where(gate > gj, 1.0, 0.0)
            if j > 0:
                ahead = ahead + jnp.where(rowj < j, jnp.where(gate == gj, 1.0, 0.0), 0.0)
            ranks.append(jnp.sum(ahead, axis=0, keepdims=True))
        rank = jnp.concatenate(ranks, axis=0)
        chosen_bias = jnp.where(rowj == qi - 1, 0.0, far_ref[hp * 2 + hh])
        addm_ref[hh, 0:nb, :] = jnp.where(rowj < qi, jnp.where(rank < float(MOBA_TOPK), chosen_bias, NEG), NEG)

    def k_rows(start, rows):
        return k_ref[0, pl.ds(pl.multiple_of(start, blk), rows), :]

    def item(i, slot, carry):
        t = jnp.minimum(i, nb // 2 - 1)
        kb = k_rows(2 * t * blk, 2 * blk)
        ss = [_dot_nt(kb, qms[hh]) for hh in range(2)]
        blocks = (jnp.where(i == 0, pc, 2 * i - 2), jnp.where(i == 0, qi, 2 * i - 1))
        add_rows = (jnp.where(i == 0, nb + 1, 2 * i - 2),
                    jnp.where(i == 0, nb + 1, jnp.where(2 * i - 1 <= qi - 2, 2 * i - 1, nb)))
        adds = [[addm_ref[hh, pl.ds(r, 1), :] for r in add_rows] for hh in range(2)]
        halves = [[s_ref[slot, hh, n * blk:(n + 1) * blk, :] for n in range(2)] for hh in range(2)]
        m_new = [functools.reduce(jnp.maximum, [carry[2 * hh]] + [
            jnp.max(halves[hh][n], axis=0, keepdims=True) + adds[hh][n] for n in range(2)]) for hh in range(2)]
        ps = [[jnp.exp(halves[hh][n] - (m_new[hh] - adds[hh][n])).astype(BF16) for n in range(2)] for hh in range(2)]
        pv = [sum(_dot(vt_ref[j, hh], ps[hh][n]) for n, j in enumerate(blocks))
              for hh in range(2)]
        out = ()
        for hh in range(2):
            alpha = jnp.exp(carry[2 * hh] - m_new[hh])
            out += (m_new[hh], alpha * carry[2 * hh + 1] + pv[hh])
        for hh in range(2):
            s_ref[1 - slot, hh] = ss[hh]
        return out

    pc = jnp.maximum(qi - 1, 0)
    k_prev, k_own = k_rows(pc * blk, blk), k_rows(qi * blk, blk)
    for hh in range(2):
        s_ref[0, hh, 0:blk, :] = _dot_nt(k_prev, qms[hh]) + bias_ref[hh, 1] + addm_ref[hh, pl.ds(pc, 1), :]
        s_ref[0, hh, blk:, :] = _dot_nt(k_own, qms[hh]) + bias_ref[hh, 0]
    n_items = jnp.maximum(qi, 1) // 2 + 1

    start = (jnp.full((1, blk), -jnp.inf, F32), jnp.zeros((hd + ONES_ROWS, blk), F32))
    carry = lax.fori_loop(0, n_items // 2, lambda t, c: item(2 * t + 1, 1, item(2 * t, 0, c)), start + start)
    carry = lax.cond(lax.rem(n_items, 2) == 1, lambda c: item(n_items - 1, 0, c), lambda c: c, carry)
    outs = [carry[2 * hh + 1][:hd] / carry[2 * hh + 1][hd:hd + 1] for hh in range(2)]
    o_ref[0, q_rows, :] = jnp.concatenate(outs, axis=0).T.astype(o_ref.dtype)
    return 0


def _moba(zqkv, bias_tiles, far_bias):
    bsz, s, _ = zqkv.shape
    nb = s // MOBA_BLOCK
    n_pairs = A_HEADS // 2
    pair_w = 2 * A_HEAD_DIM
    return pl.pallas_call(
        functools.partial(_moba_kernel, nb=nb),
        grid=(bsz, n_pairs),
        in_specs=[pl.BlockSpec(memory_space=pltpu.SMEM),
                  pl.BlockSpec((1, s, pair_w), lambda b, hp: (b, 0, hp)),
                  pl.BlockSpec((1, s, pair_w), lambda b, hp: (b, 0, n_pairs + hp)),
                  pl.BlockSpec((1, s, pair_w), lambda b, hp: (b, 0, 2 * n_pairs + hp)),
                  pl.BlockSpec((2, 2, MOBA_BLOCK, MOBA_BLOCK), lambda b, hp: (hp, 0, 0, 0))],
        out_specs=pl.BlockSpec((1, s, pair_w), lambda b, hp: (b, 0, hp)),
        out_shape=jax.ShapeDtypeStruct((bsz, s, A_WIDTH), BF16),
        scratch_shapes=[pltpu.VMEM((nb, pair_w), F32),
                        pltpu.VMEM((nb, 2, A_HEAD_DIM + ONES_ROWS, MOBA_BLOCK), BF16),
                        pltpu.VMEM((2, nb + HALO, MOBA_BLOCK), F32),
                        pltpu.VMEM((2, 2, 2 * MOBA_BLOCK, MOBA_BLOCK), F32)],
        compiler_params=pltpu.CompilerParams(dimension_semantics=("parallel", "parallel"),
                                             vmem_limit_bytes=VMEM_LIMIT),
        name="moba_attention",
    )(far_bias, zqkv, zqkv, zqkv, bias_tiles)


def _mlstm_kernel(zb_ref, zg_ref, cw_ref, cb_ref, gb_ref, ng_ref, o_ref, hist_ref, state_ref, m_ref, x_ref):
    @pl.when(pl.program_id(1) == 0)
    def _():
        hist_ref[...] = jnp.zeros_like(hist_ref)
        state_ref[...] = jnp.zeros_like(state_ref)
        m_ref[...] = jnp.zeros_like(m_ref)

    refs = (zb_ref, zg_ref, cw_ref, cb_ref, gb_ref, ng_ref, o_ref, hist_ref, state_ref, m_ref, x_ref)
    lax.fori_loop(0, zb_ref.shape[1] // CHUNK, functools.partial(_mlstm_chunk, refs), 0)


def _mlstm_chunk(refs, j, _):
    zb_ref, zg_ref, cw_ref, cb_ref, gb_ref, ng_ref, o_ref, hist_ref, state_ref, m_ref, x_ref = refs
    L = CHUNK
    dh = B_HEAD_DIM
    rows = pl.ds(pl.multiple_of(j * L, L), L)

    x = zb_ref[0, rows, 0:2 * B_WIDTH].astype(F32)
    lanes = x_ref.shape[2]
    ys = []
    for k in range(x_ref.shape[0]):
        ls = slice(k * lanes, (k + 1) * lanes)
        x_ref[k, 0:HALO, :] = hist_ref[:, ls]
        x_ref[k, HALO:, :] = x[:, ls]
        yk = cw_ref[MLSTM_CONV - 1:MLSTM_CONV, ls] * x[:, ls]
        for back in range(1, MLSTM_CONV):
            yk = yk + cw_ref[MLSTM_CONV - 1 - back:MLSTM_CONV - back, ls] * x_ref[k, HALO - back:HALO - back + L, :]
        ys.append(yk)
    hist_ref[...] = x[L - HALO:, :]
    y = jnp.concatenate(ys, axis=1) + cb_ref[...]
    y = y * jax.nn.sigmoid(y)
    q_all = y[:, :B_WIDTH]
    k_all = y[:, B_WIDTH:] * (dh ** -0.5)

    g = zg_ref[0, rows, :] + gb_ref[...]
    lane = lax.broadcasted_iota(I32, (1, 128), 1)
    log_f = jnp.minimum(g, 0.0) - jnp.log(1.0 + jnp.exp(-jnp.abs(g)))
    gates = jnp.where(lane < B_HEADS, g, log_f)
    row = lax.broadcasted_iota(I32, (L, L), 0)
    col = lax.broadcasted_iota(I32, (L, L), 1)
    causal = row >= col
    tril = jnp.where(causal, 1.0, 0.0).astype(BF16)
    triu = jnp.where(row <= col, 1.0, 0.0).astype(BF16)
    csum_col = sum(_dot(tril, part) for part in _split3(gates))
    gates_t = gates.T
    csum_row = sum(_dot(part, triu) for part in _split3(gates_t[0:16, :]))
    top_row = lax.broadcasted_iota(I32, (dh, L), 0) == 0

    heads = range(B_HEADS)
    sls = [slice(h * dh, (h + 1) * dh) for h in heads]
    qhs = [q_all[:, sls[h]].astype(BF16) for h in heads]
    khs = [k_all[:, sls[h]].astype(BF16) for h in heads]
    states = [state_ref[h] for h in heads]
    m_prevs = [m_ref[h] for h in heads]
    qk_t = [_dot_nt(khs[h], qhs[h]) for h in heads]
    inter_t = [_dot_nt(states[h].astype(BF16), qhs[h]) for h in heads]
    v_t = [zb_ref[0, rows, 2 * B_WIDTH + h * dh:2 * B_WIDTH + (h + 1) * dh].astype(F32).T for h in heads]
    ones_t = jnp.where(top_row, 1.0, 0.0).astype(BF16)
    vaug_t = [jnp.concatenate([v_t[h].astype(BF16), ones_t], axis=0) for h in heads]

    c_cols = [gates[:, h:h + 1] - csum_col[:, B_HEADS + h:B_HEADS + h + 1] for h in heads]
    b_rows = [csum_row[B_HEADS + h:B_HEADS + h + 1, :] for h in heads]
    i_rows = [gates_t[h:h + 1, :] for h in heads]
    log_inter = [b_rows[h] + m_prevs[h] for h in heads]
    log_intra = [jnp.where(row <= col, b_rows[h] + c_cols[h], -jnp.inf) for h in heads]
    m_t = [jnp.maximum(log_inter[h], jnp.max(log_intra[h], axis=0, keepdims=True)) for h in heads]
    decay = [jnp.exp(log_intra[h] - m_t[h]) for h in heads]
    w_inter = [jnp.exp(log_inter[h] - m_t[h]) for h in heads]

    nd = [w_inter[h] * inter_t[h] + _dot(vaug_t[h], (qk_t[h] * decay[h]).astype(BF16)) for h in heads]

    b_last = [b_rows[h][:, L - 1:L] for h in heads]
    log_s = [b_last[h] - b_rows[h] + i_rows[h] for h in heads]
    m_new = [jnp.maximum(b_last[h] + m_prevs[h], jnp.max(log_s[h], axis=1, keepdims=True)) for h in heads]
    w_prev = [jnp.exp(b_last[h] + m_prevs[h] - m_new[h]) for h in heads]
    w_s = [jnp.exp(log_s[h] - m_new[h]) for h in heads]
    for h in heads:
        weighted = jnp.concatenate([v_t[h] * w_s[h], jnp.where(top_row, w_s[h], 0.0)], axis=0).astype(BF16)
        state_ref[h] = w_prev[h] * states[h] + _dot(weighted, khs[h])
        m_ref[h] = m_new[h]

    for h in heads:
        den = nd[h][dh:dh + 1, :]
        hout = nd[h][:dh, :] / jnp.maximum(jnp.abs(den), jnp.exp(-m_t[h]))
        hn = hout * lax.rsqrt(jnp.mean(hout * hout, axis=0, keepdims=True) + EPS) * ng_ref[sls[h], :]
        og = jax.nn.sigmoid(zb_ref[0, rows, 3 * B_WIDTH + h * dh:3 * B_WIDTH + (h + 1) * dh].astype(F32))
        o_ref[0, rows, sls[h]] = (hn.T * og).astype(o_ref.dtype)
    return 0


def _mlstm(zb, zg, conv_w, conv_b, b_i, b_f, norm_g):
    bsz, s, _ = zb.shape
    rows = min(MIXER_ROWS, s)
    gate_bias = jnp.zeros((1, 128), F32).at[0, :B_HEADS].set(b_i.astype(F32)).at[0, B_HEADS:2 * B_HEADS].set(
        b_f.astype(F32))
    return pl.pallas_call(
        _mlstm_kernel,
        grid=(bsz, s // rows),
        in_specs=[pl.BlockSpec((1, rows, 4 * B_WIDTH), lambda b, c: (b, c, 0)),
                  pl.BlockSpec((1, rows, 128), lambda b, c: (b, c, 0)),
                  _resident((MLSTM_CONV, 2 * B_WIDTH)), _resident((1, 2 * B_WIDTH)),
                  _resident((1, 128)), _resident((B_WIDTH, 1))],
        out_specs=pl.BlockSpec((1, rows, B_WIDTH), lambda b, c: (b, c, 0)),
        out_shape=jax.ShapeDtypeStruct((bsz, s, B_WIDTH), BF16),
        scratch_shapes=[pltpu.VMEM((HALO, 2 * B_WIDTH), F32),
                        pltpu.VMEM((B_HEADS, 2 * B_HEAD_DIM, B_HEAD_DIM), F32),
                        pltpu.VMEM((B_HEADS, 1, 1), F32),
                        pltpu.VMEM((2 * B_WIDTH // 128, HALO + CHUNK, 128), F32)],
        compiler_params=pltpu.CompilerParams(dimension_semantics=("parallel", "arbitrary"),
                                             vmem_limit_bytes=VMEM_LIMIT),
        name="mlstm",
    )(zb, zg.reshape(bsz, s, 128), conv_w.astype(F32), conv_b.reshape(1, -1).astype(F32), gate_bias,
      norm_g.reshape(-1, 1).astype(F32))


def _pool_ret_kernel(z_ref, cos_ref, sin_ref, swap_ref, dq_ref, dk_ref, dc_ref, dm_ref, bd_ref, pw_ref, ps_ref,
                     rg_ref, o_ref, hist_ref, state_ref, win_ref):
    @pl.when(pl.program_id(1) == 0)
    def _():
        hist_ref[...] = jnp.zeros_like(hist_ref)
        state_ref[...] = jnp.zeros_like(state_ref)

    refs = (z_ref, cos_ref, sin_ref, swap_ref, dq_ref, dk_ref, dc_ref, dm_ref, bd_ref, pw_ref, ps_ref, rg_ref, o_ref,
            hist_ref, state_ref, win_ref)
    lax.fori_loop(0, z_ref.shape[1] // CHUNK, functools.partial(_pool_ret_chunk, refs), 0)


def _pool_ret_chunk(refs, j, _):
    (z_ref, cos_ref, sin_ref, swap_ref, dq_ref, dk_ref, dc_ref, dm_ref, bd_ref, pw_ref, ps_ref, rg_ref, o_ref,
     hist_ref, state_ref, win_ref) = refs
    L = CHUNK
    rows = pl.ds(pl.multiple_of(j * L, L), L)
    c = pl.program_id(1) * (z_ref.shape[1] // L) + j

    u = z_ref[0, rows, 0:POOL_WIDTH].astype(F32)
    pos1 = (c * L + lax.broadcasted_iota(I32, (L, 1), 0) + 1).astype(F32)
    gd = POOL_GROUP_DIM
    for g, w in enumerate(POOL_WINDOWS):
        sl = slice(g * gd, (g + 1) * gd)
        w_ref = win_ref.at[g]
        w_ref[0:POOL_HALO, :] = hist_ref[:, sl]
        w_ref[POOL_HALO:, :] = u[:, sl]
        span = 1
        while span < w:
            n_sum = POOL_HALO + L - span
            w_ref[span:, :] = w_ref[span:span + n_sum, :] + w_ref[0:n_sum, :]
            span *= 2
        pooled = w_ref[POOL_HALO:, :] / jnp.minimum(pos1, float(w)) - u[:, sl]
        o_ref[0, rows, sl] = (_dot(pooled.astype(BF16), pw_ref[g]) * ps_ref[:, sl]).astype(o_ref.dtype)

    rq = z_ref[0, rows, POOL_WIDTH:POOL_WIDTH + R_QK_WIDTH]
    rk = z_ref[0, rows, POOL_WIDTH + R_QK_WIDTH:POOL_WIDTH + 2 * R_QK_WIDTH]
    v0 = POOL_WIDTH + 2 * R_QK_WIDTH
    rv = z_ref[0, rows, v0:v0 + R_V_WIDTH]
    cosv, sinv = cos_ref[rows, :], sin_ref[rows, :]
    q_rot = rq.astype(F32) * cosv + _dot(rq, swap_ref[...]) * sinv
    k_rot = (rk.astype(F32) * cosv + _dot(rk, swap_ref[...]) * sinv) * (R_QK_DIM ** -0.5)
    qb = q_rot.astype(BF16)
    kb = k_rot.astype(BF16)
    inter = _dot((q_rot * dq_ref[...]).astype(BF16), state_ref[...].astype(BF16))
    lane = lax.broadcasted_iota(I32, (1, R_QK_WIDTH), 1)
    g0 = v0 + R_V_WIDTH
    for h in range(R_HEADS):
        in_head = (lane >= h * R_QK_DIM) & (lane < (h + 1) * R_QK_DIM)
        sc = _dot_nt(jnp.where(in_head, qb, jnp.zeros_like(qb)), kb) * dm_ref[h]
        vs = slice(h * R_V_DIM, (h + 1) * R_V_DIM)
        y = _dot(sc.astype(BF16), rv[:, vs]) + inter[:, vs]
        gate = z_ref[0, rows, g0 + h * R_V_DIM:g0 + (h + 1) * R_V_DIM].astype(F32)
        gate = gate * jax.nn.sigmoid(gate)
        o_ref[0, rows, POOL_WIDTH + h * R_V_DIM:POOL_WIDTH + (h + 1) * R_V_DIM] = (
            _rms(y, rg_ref[:, vs]) * gate).astype(o_ref.dtype)
    hist_ref[...] = u[L - POOL_HALO:, :]
    upd = _dot((k_rot * dk_ref[...]).T.astype(BF16), rv)
    state_ref[...] = dc_ref[...] * state_ref[...] + upd * bd_ref[...]
    return 0


def _retention_tables(s):
    L = CHUNK
    half = R_QK_DIM // 2
    inv = ROPE_BASE ** (-jnp.arange(half, dtype=F32) / half)
    ang = jnp.arange(s, dtype=F32)[:, None] * inv[None, :]
    cos, sin = jnp.cos(ang), jnp.sin(ang)
    cos_t = jnp.tile(jnp.concatenate([cos, cos], axis=-1), (1, R_HEADS))
    sin_t = jnp.tile(jnp.concatenate([-sin, sin], axis=-1), (1, R_HEADS))
    j = np.arange(R_QK_WIDTH)
    src = (j // R_QK_DIM) * R_QK_DIM + (j % R_QK_DIM + half) % R_QK_DIM
    swap = np.zeros((R_QK_WIDTH, R_QK_WIDTH), np.float32)
    swap[src, j] = 1.0
    log_g = jnp.log(1.0 - jnp.exp2(-5.0 - jnp.arange(R_HEADS, dtype=F32)))
    t = jnp.arange(L, dtype=F32)
    rel = t[:, None] - t[None, :]
    decay_intra = jnp.where(rel >= 0, jnp.exp(jnp.maximum(rel, 0.0) * log_g[:, None, None]), 0.0)
    decay_q = jnp.exp((t + 1.0) * log_g[:, None])
    decay_k = jnp.exp((L - 1.0 - t) * log_g[:, None])
    decay_chunk = jnp.exp(L * log_g)
    dq = jnp.repeat(decay_q.T, R_QK_DIM, axis=1)
    dk = jnp.repeat(decay_k.T, R_QK_DIM, axis=1)
    dc = jnp.repeat(decay_chunk, R_QK_DIM)[:, None]
    bd = (np.arange(R_QK_WIDTH)[:, None] // R_QK_DIM == np.arange(R_V_WIDTH)[None, :] // R_V_DIM)
    return cos_t, sin_t, jnp.asarray(swap, BF16), dq, dk, dc, decay_intra, jnp.asarray(bd, F32)


def _pool_ret(z, pool_w, pool_scale, ret_g):
    bsz, s, width = z.shape
    cos_t, sin_t, swap, dq, dk, dc, dm, bd = _retention_tables(s)
    rows = min(MIXER_ROWS, s)
    return pl.pallas_call(
        _pool_ret_kernel,
        grid=(bsz, s // rows),
        in_specs=[pl.BlockSpec((1, rows, width), lambda b, c: (b, c, 0)),
                  pl.BlockSpec((rows, R_QK_WIDTH), lambda b, c: (c, 0)),
                  pl.BlockSpec((rows, R_QK_WIDTH), lambda b, c: (c, 0)),
                  _resident(swap.shape), _resident(dq.shape), _resident(dk.shape), _resident(dc.shape),
                  _resident(dm.shape), _resident(bd.shape), _resident(pool_w.shape),
                  _resident((1, POOL_WIDTH)), _resident((1, R_V_WIDTH))],
        out_specs=pl.BlockSpec((1, rows, POOL_WIDTH + R_V_WIDTH), lambda b, c: (b, c, 0)),
        out_shape=jax.ShapeDtypeStruct((bsz, s, POOL_WIDTH + R_V_WIDTH), BF16),
        scratch_shapes=[pltpu.VMEM((POOL_HALO, POOL_WIDTH), F32),
                        pltpu.VMEM((R_QK_WIDTH, R_V_WIDTH), F32),
                        pltpu.VMEM((len(POOL_WINDOWS), POOL_HALO + CHUNK, POOL_GROUP_DIM), F32)],
        compiler_params=pltpu.CompilerParams(dimension_semantics=("parallel", "arbitrary"),
                                             vmem_limit_bytes=VMEM_LIMIT),
        name="pool_retention",
    )(z, cos_t, sin_t, swap, dq, dk, dc, dm, bd, pool_w.astype(BF16), pool_scale.reshape(1, -1).astype(F32),
      ret_g.reshape(1, -1).astype(F32))


def _ffn_kernel(h_ref, a1_ref, a2_ref, wo_ref, g_ref, wup_ref, cw_ref, cb_ref, wd_ref, gout_ref, o_ref,
                hist_ref, y_ref, x_ref, a_ref, *, n_tiles, final_norm):
    tm = h_ref.shape[1]
    lanes = x_ref.shape[2]
    n_lane_tiles = FF_TILE // lanes

    @pl.when(pl.program_id(1) == 0)
    def _():
        hist_ref[...] = jnp.zeros_like(hist_ref)

    half = a1_ref.shape[2]
    h1 = h_ref[0] + _dot(a1_ref[0], wo_ref[0:half, :]) + _dot(a2_ref[0], wo_ref[half:, :])
    y_ref[...] = _rms(h1, g_ref[...]).astype(BF16)
    o_ref[0] = h1

    def cols(idx):
        return pl.ds(pl.multiple_of(idx * FF_TILE, FF_TILE), FF_TILE)

    n_parts = 4
    rp = tm // n_parts

    def conv(u, idx, slot, part):
        w = cw_ref[:, cols(idx)]
        r0 = HALO + part * rp
        outs = []
        for k in range(n_lane_tiles):
            ls = slice(k * lanes, (k + 1) * lanes)
            xr = x_ref.at[slot * n_lane_tiles + k]
            if part == 0:
                xr[0:HALO, :] = hist_ref[idx, :, ls]
            xr[r0:r0 + rp, :] = u[:, ls]
            if part == n_parts - 1:
                hist_ref[idx, :, ls] = xr[tm:tm + HALO, :]
            outs.append(w[2:3, ls] * xr[r0:r0 + rp, :] + w[1:2, ls] * xr[r0 - 1:r0 - 1 + rp, :]
                        + w[0:1, ls] * xr[r0 - 2:r0 - 2 + rp, :])
        return jnp.concatenate(outs, axis=1) + cb_ref[:, cols(idx)]

    def tile(c, _):
        slot = 2 * lax.rem(c, 2)
        for part in range(n_parts):
            y = y_ref[part * rp:(part + 1) * rp, :]
            gate = conv(_dot(y, wup_ref[:, cols(c)]), c, slot, part)
            val = conv(_dot(y, wup_ref[:, cols(n_tiles + c)]), n_tiles + c, slot + 1, part)
            a_ref[part * rp:(part + 1) * rp, cols(c)] = (gate * jax.nn.sigmoid(gate) * val).astype(BF16)
        return 0

    lax.fori_loop(0, n_tiles, tile, 0, unroll=2)
    o_ref[0] += _dot(a_ref[...], wd_ref[...])
    if final_norm:
        o_ref[0] = _rms(o_ref[0], gout_ref[...])


def _ffn(h, a1, a2, col1, col2, layer, w_out, norm_g, w_up, conv_w, conv_b, w_down, out_g, final_norm):
    bsz, s, d = h.shape
    tm = min(FFN_ROWS, s)
    half = w_out.shape[0] // 2
    n_tiles = D_FF // FF_TILE
    wo = w_out.astype(BF16)
    g = norm_g.astype(F32).reshape(norm_g.shape[0], 1, d)
    wup = w_up.astype(BF16)
    cw = conv_w.astype(F32)
    cb = conv_b.astype(F32).reshape(conv_b.shape[0], 1, -1)
    wd = w_down.astype(BF16)
    return pl.pallas_call(
        functools.partial(_ffn_kernel, n_tiles=n_tiles, final_norm=final_norm),
        grid=(bsz, s // tm),
        in_specs=[pl.BlockSpec((1, tm, d), lambda b, t: (b, t, 0)),
                  pl.BlockSpec((1, tm, half), lambda b, t: (b, t, col1)),
                  pl.BlockSpec((1, tm, half), lambda b, t: (b, t, col2)),
                  _resident((2 * half, d)), _resident_layer(g, layer),
                  _resident_layer(wup, layer), _resident_layer(cw, layer), _resident_layer(cb, layer),
                  _resident_layer(wd, layer), _resident((1, d))],
        out_specs=pl.BlockSpec((1, tm, d), lambda b, t: (b, t, 0)),
        out_shape=jax.ShapeDtypeStruct((bsz, s, d), F32),
        scratch_shapes=[pltpu.VMEM((2 * n_tiles, HALO, FF_TILE), F32),
                        pltpu.VMEM((tm, d), BF16),
                        pltpu.VMEM((4 * FF_TILE // 128, HALO + tm, 128), F32),
                        pltpu.VMEM((tm, D_FF), BF16)],
        compiler_params=pltpu.CompilerParams(dimension_semantics=("parallel", "arbitrary"),
                                             vmem_limit_bytes=VMEM_LIMIT),
        name="out_proj_ffn",
    )(h, a1, a2, wo, g, wup, cw, cb, wd, out_g.reshape(1, d).astype(F32))


def kernel(x, rel_bias, norm_mix_g, norm_ffn_g, norm_out_g, w_in_ab, w_out_ab, mlstm_conv_w, mlstm_conv_b, mlstm_b_i,
           mlstm_b_f, mlstm_norm_g, w_in_cd, w_out_cd, pool_w, pool_scale, ret_norm_g, ffn_w_up, ffn_conv_w,
           ffn_conv_b, ffn_w_down):
    bsz, s, d = x.shape
    m = bsz * s

    n_attn = 3 * A_WIDTH
    n_main = n_attn + 4 * B_WIDTH
    n_gate = 128
    w0 = jnp.pad(w_in_ab[0].astype(BF16), ((0, 0), (0, n_main + n_gate - w_in_ab.shape[2])))
    zqkv, zb, zg = _norm_proj(x.reshape(m, d), norm_mix_g[0], w0, [n_attn, n_main - n_attn, n_gate],
                              [BF16, BF16, F32])
    bias_tiles = _bias_tables(rel_bias)
    far_bias = rel_bias[REL_BUCKETS - 1].astype(F32)
    ya = _moba(zqkv.reshape(bsz, s, n_attn), bias_tiles, far_bias)
    yb = _mlstm(zb.reshape(bsz, s, 4 * B_WIDTH), zg, mlstm_conv_w[0], mlstm_conv_b[0], mlstm_b_i[0], mlstm_b_f[0],
                mlstm_norm_g[0])
    h = _ffn(x, ya, yb, 0, 0, 0, w_out_ab[0], norm_ffn_g, ffn_w_up, ffn_conv_w, ffn_conv_b, ffn_w_down, norm_out_g, False)

    (z1,) = _norm_proj(h.reshape(m, d), norm_mix_g[1], w_in_cd[0].astype(BF16), [w_in_cd.shape[2]], [BF16])
    cat = _pool_ret(z1.reshape(bsz, s, -1), pool_w[0], pool_scale[0], ret_norm_g[0])
    return _ffn(h, cat, cat, 0, 1, 1, w_out_cd[0], norm_ffn_g, ffn_w_up, ffn_conv_w, ffn_conv_b, ffn_w_down, norm_out_g,
                True)
```
